```python
import math
import jax, jax.numpy as jnp
from jax import lax
import numpy as np

D_MODEL = 1024
BATCH = 2
SEQ = 16384
DEPTH = 1

N_HEADS = 8
HEAD_DIM = 64
V_DIM = 2 * HEAD_DIM
ROPE_THETA = 10000.0
Q_BLOCK = 128
SSM_GROUP = 16
SSM_GROUPS = 32
SSM_WIDTH = SSM_GROUP * SSM_GROUPS
SSM_STATE = 64
DT_MIN = 0.001
DT_MAX = 0.1
D_FF = 2816
CONV_WIDTH = 3
EPS = 1e-6
QK_COLS = N_HEADS * 2 * HEAD_DIM
V_COLS = N_HEADS * V_DIM
GATE_COLS = 2 * D_MODEL
IN_COLS = 2 * QK_COLS + V_COLS + SSM_WIDTH + GATE_COLS

kernel_name = "hybrid_diffattn_s5_gated_convffn"


def rmsnorm(x, g):
    xf = x.astype(jnp.float32)
    xf = xf * lax.rsqrt(jnp.mean(xf * xf, axis=-1, keepdims=True) + EPS)
    return (xf * g.astype(jnp.float32)).astype(x.dtype)


def rope(t, cos, sin):
    tf = t.astype(jnp.float32)
    t1, t2 = jnp.split(tf, 2, axis=-1)
    out = jnp.concatenate([t1 * cos - t2 * sin, t2 * cos + t1 * sin], axis=-1)
    return out.astype(t.dtype)


def diff_attention(q, k, v, lam):
    bsz, seq = q.shape[0], q.shape[1]
    nb = seq // Q_BLOCK
    scale = 1.0 / math.sqrt(HEAD_DIM)
    qb = q.reshape(bsz, nb, Q_BLOCK, N_HEADS, 2, HEAD_DIM).transpose(1, 0, 2, 3, 4, 5)
    kpos = jnp.arange(seq)
    neg = jnp.finfo(jnp.float32).min

    def one_block(args):
        qblk, bi = args
        s = jnp.einsum('bqhcd,bkhcd->bhcqk', qblk, k).astype(jnp.float32) * scale
        qpos = bi * Q_BLOCK + jnp.arange(Q_BLOCK)
        mask = kpos[None, :] <= qpos[:, None]
        s = jnp.where(mask, s, neg)
        p = jax.nn.softmax(s, axis=-1)
        a = p[:, :, 0] - lam * p[:, :, 1]
        return jnp.einsum('bhqk,bkhe->bqhe', a.astype(v.dtype), v)

    o = lax.map(one_block, (qb, jnp.arange(nb)))
    return o.transpose(1, 0, 2, 3, 4).reshape(bsz, seq, N_HEADS, V_DIM)


def _complex_affine_combine(e1, e2):
    a1r, a1i, b1r, b1i = e1
    a2r, a2i, b2r, b2i = e2
    ar = a2r * a1r - a2i * a1i
    ai = a2r * a1i + a2i * a1r
    br = a2r * b1r - a2i * b1i + b2r
    bi = a2r * b1i + a2i * b1r + b2i
    return ar, ai, br, bi


def s5_layer(u, a_re, a_im, log_dt, b_re, b_im, c_re, c_im, d_skip, w_glu, b_glu):
    dtype = u.dtype
    bsz, seq = u.shape[0], u.shape[1]
    f32 = jnp.float32
    uf = u.astype(f32).reshape(bsz, seq, SSM_GROUPS, SSM_GROUP)
    a_re = a_re.astype(f32); a_im = a_im.astype(f32)
    dt = jnp.exp(log_dt.astype(f32))[:, None]
    mag = jnp.exp(a_re * dt)
    lb_re = mag * jnp.cos(a_im * dt)
    lb_im = mag * jnp.sin(a_im * dt)
    nr, ni = lb_re - 1.0, lb_im
    den = a_re * a_re + a_im * a_im
    cr = (nr * a_re + ni * a_im) / den
    ci = (ni * a_re - nr * a_im) / den
    b_re = b_re.astype(f32); b_im = b_im.astype(f32)
    bb_re = cr[..., None] * b_re - ci[..., None] * b_im
    bb_im = cr[..., None] * b_im + ci[..., None] * b_re
    bu_re = jnp.einsum('blgc,gpc->blgp', uf, bb_re)
    bu_im = jnp.einsum('blgc,gpc->blgp', uf, bb_im)
    ar = jnp.broadcast_to(lb_re, bu_re.shape)
    ai = jnp.broadcast_to(lb_im, bu_im.shape)
    _, _, xr, xi = lax.associative_scan(_complex_affine_combine, (ar, ai, bu_re, bu_im), axis=1)
    y = (jnp.einsum('gcp,blgp->blgc', c_re.astype(f32), xr)
         - jnp.einsum('gcp,blgp->blgc', c_im.astype(f32), xi)
         + d_skip.astype(f32) * uf)
    y = y.reshape(bsz, seq, SSM_WIDTH)
    g = jax.nn.gelu(y)
    out = g * jax.nn.sigmoid(g @ w_glu.astype(f32) + b_glu.astype(f32))
    return out.astype(dtype)


def causal_dwconv(u, w, b):
    c = u.shape[-1]
    out = lax.conv_general_dilated(
        u, w.astype(u.dtype)[:, None, :], window_strides=(1,),
        padding=[(CONV_WIDTH - 1, 0)], dimension_numbers=('NWC', 'WIO', 'NWC'),
        feature_group_count=c)
    return out + b.astype(u.dtype)


def setup_inputs(seed: int = 0) -> dict:
    key = jax.random.key(seed)
    ks = iter(jax.random.split(key, 40))

    def nrm(shape, scale):
        return jax.random.normal(next(ks), shape, jnp.float32) * scale

    def gain(shape):
        return 1.0 + nrm(shape, 0.02)

    L = DEPTH
    x = nrm((BATCH, SEQ, D_MODEL), 1.0)
    norm1_gain = gain((L, D_MODEL))
    w_in = nrm((L, D_MODEL, IN_COLS), D_MODEL ** -0.5)
    q_norm_gain = gain((L, HEAD_DIM))
    k_norm_gain = gain((L, HEAD_DIM))
    lambda_q1 = nrm((L, HEAD_DIM), 0.1)
    lambda_k1 = nrm((L, HEAD_DIM), 0.1)
    lambda_q2 = nrm((L, HEAD_DIM), 0.1)
    lambda_k2 = nrm((L, HEAD_DIM), 0.1)
    subln_gain = gain((L, V_DIM))
    w_attn_proj = nrm((L, V_COLS, D_MODEL), V_COLS ** -0.5)
    ssm_a_re = -0.5 * (1.0 + nrm((L, SSM_GROUPS, SSM_STATE), 0.02))
    ssm_a_im = (math.pi * jnp.arange(SSM_STATE, dtype=jnp.float32))[None, None, :] + nrm((L, SSM_GROUPS, SSM_STATE), 0.02)
    ssm_log_dt = jax.random.uniform(next(ks), (L, SSM_GROUPS), jnp.float32,
                                    math.log(DT_MIN), math.log(DT_MAX))
    ssm_b_re = nrm((L, SSM_GROUPS, SSM_STATE, SSM_GROUP), (2 * SSM_GROUP) ** -0.5)
    ssm_b_im = nrm((L, SSM_GROUPS, SSM_STATE, SSM_GROUP), (2 * SSM_GROUP) ** -0.5)
    ssm_c_re = nrm((L, SSM_GROUPS, SSM_GROUP, SSM_STATE), (2 * SSM_STATE) ** -0.5)
    ssm_c_im = nrm((L, SSM_GROUPS, SSM_GROUP, SSM_STATE), (2 * SSM_STATE) ** -0.5)
    ssm_d = nrm((L, SSM_GROUPS, SSM_GROUP), 1.0)
    w_glu = nrm((L, SSM_WIDTH, SSM_WIDTH), SSM_WIDTH ** -0.5)
    b_glu = nrm((L, SSM_WIDTH), 0.02)
    w_ssm_proj = nrm((L, SSM_WIDTH, D_MODEL), SSM_WIDTH ** -0.5)
    w_out = nrm((L, D_MODEL, D_MODEL), D_MODEL ** -0.5)
    norm2_gain = gain((L, D_MODEL))
    w_up = nrm((L, D_MODEL, 2 * D_FF), D_MODEL ** -0.5)
    conv_w = nrm((L, CONV_WIDTH, 2 * D_FF), CONV_WIDTH ** -0.5)
    conv_b = nrm((L, 2 * D_FF), 0.02)
    w_down = nrm((L, D_FF, D_MODEL), D_FF ** -0.5)
    return {"x": x, "norm1_gain": norm1_gain, "w_in": w_in,
            "q_norm_gain": q_norm_gain, "k_norm_gain": k_norm_gain,
            "lambda_q1": lambda_q1, "lambda_k1": lambda_k1,
            "lambda_q2": lambda_q2, "lambda_k2": lambda_k2,
            "subln_gain": subln_gain, "w_attn_proj": w_attn_proj,
            "ssm_a_re": ssm_a_re, "ssm_a_im": ssm_a_im, "ssm_log_dt": ssm_log_dt,
            "ssm_b_re": ssm_b_re, "ssm_b_im": ssm_b_im,
            "ssm_c_re": ssm_c_re, "ssm_c_im": ssm_c_im, "ssm_d": ssm_d,
            "w_glu": w_glu, "b_glu": b_glu, "w_ssm_proj": w_ssm_proj,
            "w_out": w_out, "norm2_gain": norm2_gain, "w_up": w_up,
            "conv_w": conv_w, "conv_b": conv_b, "w_down": w_down}


def reference(x, norm1_gain, w_in, q_norm_gain, k_norm_gain, lambda_q1, lambda_k1,
              lambda_q2, lambda_k2, subln_gain, w_attn_proj, ssm_a_re, ssm_a_im,
              ssm_log_dt, ssm_b_re, ssm_b_im, ssm_c_re, ssm_c_im, ssm_d, w_glu, b_glu,
              w_ssm_proj, w_out, norm2_gain, w_up, conv_w, conv_b, w_down):
    bsz, seq, _ = x.shape
    f32 = jnp.float32
    pos = jnp.arange(seq, dtype=f32)
    inv_freq = 1.0 / (ROPE_THETA ** (jnp.arange(0, HEAD_DIM, 2, dtype=f32) / HEAD_DIM))
    ang = pos[:, None] * inv_freq[None, :]
    cos = jnp.cos(ang)[None, :, None, None, :]
    sin = jnp.sin(ang)[None, :, None, None, :]
    splits = [QK_COLS, 2 * QK_COLS, 2 * QK_COLS + V_COLS, 2 * QK_COLS + V_COLS + SSM_WIDTH]

    h = x
    for i in range(DEPTH):
        lam_init = 0.8 - 0.6 * math.exp(-0.3 * i)
        xn = rmsnorm(h, norm1_gain[i])
        proj = xn @ w_in[i]
        q, k, v, u, gates = jnp.split(proj, splits, axis=-1)
        q = q.reshape(bsz, seq, N_HEADS, 2, HEAD_DIM)
        k = k.reshape(bsz, seq, N_HEADS, 2, HEAD_DIM)
        v = v.reshape(bsz, seq, N_HEADS, V_DIM)
        q = rope(rmsnorm(q, q_norm_gain[i]), cos, sin)
        k = rope(rmsnorm(k, k_norm_gain[i]), cos, sin)
        lam = (jnp.exp(jnp.sum(lambda_q1[i].astype(f32) * lambda_k1[i].astype(f32)))
               - jnp.exp(jnp.sum(lambda_q2[i].astype(f32) * lambda_k2[i].astype(f32)))
               + lam_init)
        o = diff_attention(q, k, v, lam)
        o = rmsnorm(o, subln_gain[i]) * (1.0 - lam_init)
        attn_branch = o.reshape(bsz, seq, V_COLS) @ w_attn_proj[i]
        s = s5_layer(u, ssm_a_re[i], ssm_a_im[i], ssm_log_dt[i], ssm_b_re[i], ssm_b_im[i],
                     ssm_c_re[i], ssm_c_im[i], ssm_d[i], w_glu[i], b_glu[i])
        ssm_branch = s @ w_ssm_proj[i]
        g_attn, g_ssm = jnp.split(jax.nn.sigmoid(gates), 2, axis=-1)
        h = h + (g_attn * attn_branch + g_ssm * ssm_branch) @ w_out[i]
        hn = rmsnorm(h, norm2_gain[i])
        up = causal_dwconv(hn @ w_up[i], conv_w[i], conv_b[i])
        gate, val = jnp.split(up, 2, axis=-1)
        h = h + (jax.nn.silu(gate) * val) @ w_down[i]
    return h
```

```python
import functools
import math

import numpy as np
import jax
import jax.numpy as jnp
from jax import lax
from jax.experimental import pallas as pl
from jax.experimental.pallas import tpu as pltpu

F32 = jnp.float32
BF16 = jnp.bfloat16

N_HEADS = 8
HEAD_DIM = 64
HEAD_COLS = 2 * HEAD_DIM
ROPE_HALF = HEAD_DIM // 2
V_DIM = 128
ROPE_THETA = 10000.0
SSM_GROUP = 16
SSM_GROUPS = 32
SSM_STATE = 64
SSM_CHUNK = 16
CONV_WIDTH = 3
EPS = 1e-6
LAM_INIT = 0.8 - 0.6 * math.exp(-0.3 * 0)

LANES = 128
SUBLANES = 8
TOKEN_TILE = 512
FFN_CHUNK = 256
VMEM_LIMIT = 56 * 1024 * 1024
MASK_VALUE = -1e30
LOG2E = math.log2(math.e)


def _dot(a, b):
    return jnp.dot(a, b, preferred_element_type=F32)


def _rms_scale(x):
    return lax.rsqrt(jnp.mean(x * x, axis=-1, keepdims=True) + EPS)


def _resident(shape):
    return pl.BlockSpec(shape, lambda *_: (0,) * len(shape), pipeline_mode=pl.Buffered(1))


def _head_lane_component():
    j = np.arange(HEAD_COLS)
    return (j // ROPE_HALF) % 2


def _qk_column_permutation():
    j = np.arange(HEAD_COLS)
    comp = _head_lane_component()
    d = (j // HEAD_DIM) * ROPE_HALF + j % ROPE_HALF
    within = comp * HEAD_DIM + d
    return (np.arange(N_HEADS)[:, None] * HEAD_COLS + within[None, :]).reshape(-1), d


def _inproj_body(x_ref, g1_ref, w_ref, gq_ref, gk_ref, cos_ref, sin_ref, e_ref, et_ref,
                 qT_ref, k_ref, vT_ref, u_ref, sg_ref, *, d_qk, d_v, d_u):
    x = x_ref[0]
    xn = (x * _rms_scale(x) * g1_ref[...]).astype(BF16)
    cos = cos_ref[...]
    sin = sin_ref[...]

    def norm_rope(col0, gain_ref):
        t = _dot(xn, w_ref[:, col0:col0 + d_qk])
        ss = _dot((t * t).astype(BF16), e_ref[...])
        r = lax.rsqrt(ss * (1.0 / HEAD_DIM) + EPS)
        r_hi = r.astype(BF16)
        r_lo = (r - r_hi.astype(F32)).astype(BF16)
        rf = _dot(r_hi, et_ref[...]) + _dot(r_lo, et_ref[...])
        t = t * rf * gain_ref[...]
        heads = []
        for h in range(N_HEADS):
            th = t[:, h * HEAD_COLS:(h + 1) * HEAD_COLS]
            heads.append(th * cos + pltpu.roll(th, HEAD_DIM, axis=1) * sin)
        return jnp.concatenate(heads, axis=1)

    q = norm_rope(0, gq_ref)
    qT_ref[0, 0] = q.T.astype(BF16)
    k = norm_rope(d_qk, gk_ref)
    k_ref[0] = k.astype(BF16)
    c0 = 2 * d_qk
    v = _dot(xn, w_ref[:, c0:c0 + d_v])
    vT_ref[0, 0] = v.T.astype(BF16)
    c0 += d_v
    u_ref[0] = _dot(xn, w_ref[:, c0:c0 + d_u])
    c0 += d_u
    sg_ref[0] = jax.nn.sigmoid(_dot(xn, w_ref[:, c0:]))


def _inproj(x, g1, w_in_b, gq, gk, cos_t, sin_t, e_mat, et_mat):
    bsz, seq, d_model = x.shape
    tm = TOKEN_TILE
    nt = seq // tm
    d_qk = N_HEADS * HEAD_COLS
    d_v = N_HEADS * V_DIM
    d_u = SSM_GROUPS * SSM_GROUP
    d_g = 2 * d_model
    in_cols = w_in_b.shape[1]
    assert in_cols == 2 * d_qk + d_v + d_u + d_g and seq % tm == 0
    body = functools.partial(_inproj_body, d_qk=d_qk, d_v=d_v, d_u=d_u)
    return pl.pallas_call(
        body,
        name="inproj",
        grid=(bsz, nt),
        in_specs=[
            pl.BlockSpec((1, tm, d_model), lambda b, i: (b, i, 0)),
            _resident((1, d_model)),
            _resident((d_model, in_cols)),
            _resident((1, d_qk)),
            _resident((1, d_qk)),
            pl.BlockSpec((tm, LANES), lambda b, i: (i, 0)),
            pl.BlockSpec((tm, LANES), lambda b, i: (i, 0)),
            _resident((d_qk, LANES)),
            _resident((LANES, d_qk)),
        ],
        out_specs=[
            pl.BlockSpec((1, 1, d_qk, tm), lambda b, i: (b, i, 0, 0)),
            pl.BlockSpec((1, tm, d_qk), lambda b, i: (b, i, 0)),
            pl.BlockSpec((1, 1, d_v, tm), lambda b, i: (b, i, 0, 0)),
            pl.BlockSpec((1, tm, d_u), lambda b, i: (b, i, 0)),
            pl.BlockSpec((1, tm, d_g), lambda b, i: (b, i, 0)),
        ],
        out_shape=[
            jax.ShapeDtypeStruct((bsz, nt, d_qk, tm), BF16),
            jax.ShapeDtypeStruct((bsz, seq, d_qk), BF16),
            jax.ShapeDtypeStruct((bsz, nt, d_v, tm), BF16),
            jax.ShapeDtypeStruct((bsz, seq, d_u), F32),
            jax.ShapeDtypeStruct((bsz, seq, d_g), F32),
        ],
        compiler_params=pltpu.CompilerParams(
            dimension_semantics=("arbitrary", "arbitrary"), vmem_limit_bytes=VMEM_LIMIT),
    )(x, g1, w_in_b, gq, gk, cos_t, sin_t, e_mat, et_mat)


def _attn_body(lam_ref, qT_ref, k_ref, vT_ref, gsub_ref, o_ref, acc0_ref, acc1_ref, *, blk):
    qi = pl.program_id(2)
    qT = qT_ref[0, 0]
    row = lax.broadcasted_iota(jnp.int32, qT.shape, 0)
    is_c0 = ((row // ROPE_HALF) % 2) == 0
    zero = jnp.zeros_like(qT)
    q_comp = (jnp.where(is_c0, qT, zero), jnp.where(is_c0, zero, qT))
    accs = (acc0_ref, acc1_ref)
    acc0_ref[...] = jnp.zeros_like(acc0_ref)
    acc1_ref[...] = jnp.zeros_like(acc1_ref)

    def block(j, carry, masked):
        kj = k_ref[0, pl.ds(pl.multiple_of(j * blk, blk), blk), :]
        vj = vT_ref[0, j]
        out = []
        for c in range(2):
            m, l = carry[c]
            sT = _dot(kj, q_comp[c])
            if masked:
                kpos = lax.broadcasted_iota(jnp.int32, sT.shape, 0)
                qpos = lax.broadcasted_iota(jnp.int32, sT.shape, 1)
                sT = jnp.where(kpos <= qpos, sT, MASK_VALUE)
            m_new = jnp.maximum(m, jnp.max(sT, axis=0, keepdims=True))
            alpha = jnp.exp2(m - m_new)
            p = jnp.exp2(sT - m_new)
            l_new = alpha * l + jnp.sum(p, axis=0, keepdims=True)
            accs[c][...] = accs[c][...] * alpha + _dot(vj, p.astype(BF16))
            out.append((m_new, l_new))
        return tuple(out)

    stat0 = (jnp.full((1, blk), MASK_VALUE, F32), jnp.zeros((1, blk), F32))
    carry = lax.fori_loop(0, qi, lambda j, c: block(j, c, False), (stat0, stat0))
    (_, l0), (_, l1) = block(qi, carry, True)

    lam = lam_ref[0, 0]
    a = acc0_ref[...] * (1.0 / l0) - lam * (acc1_ref[...] * (1.0 / l1))
    a = a * lax.rsqrt(jnp.mean(a * a, axis=0, keepdims=True) + EPS)
    a = a * gsub_ref[...] * (1.0 - LAM_INIT)
    o_ref[0] = a.T.astype(BF16)


def _attention(lam, qT, k, vT, gsub):
    bsz, nblk, d_qk, blk = qT.shape
    seq = k.shape[1]
    body = functools.partial(_attn_body, blk=blk)
    return pl.pallas_call(
        body,
        name="attention",
        grid=(bsz, N_HEADS, nblk),
        in_specs=[
            pl.BlockSpec(memory_space=pltpu.SMEM),
            pl.BlockSpec((1, 1, HEAD_COLS, blk), lambda b, h, i: (b, i, h, 0)),
            pl.BlockSpec((1, seq, HEAD_COLS), lambda b, h, i: (b, 0, h)),
            pl.BlockSpec((1, nblk, V_DIM, blk), lambda b, h, i: (b, 0, h, 0)),
            pl.BlockSpec((V_DIM, 1), lambda b, h, i: (0, 0)),
        ],
        out_specs=pl.BlockSpec((1, blk, V_DIM), lambda b, h, i: (b, i, h)),
        out_shape=jax.ShapeDtypeStruct((bsz, seq, N_HEADS * V_DIM), BF16),
        scratch_shapes=[pltpu.VMEM((V_DIM, blk), F32), pltpu.VMEM((V_DIM, blk), F32)],
        compiler_params=pltpu.CompilerParams(
            dimension_semantics=("arbitrary", "arbitrary", "arbitrary"), vmem_limit_bytes=VMEM_LIMIT),
    )(lam, qT, k, vT, gsub)


def _gelu_tanh(y):
    return 0.5 * y * (1.0 + jnp.tanh(math.sqrt(2.0 / math.pi) * (y + 0.044715 * (y * y * y))))


def _s5_body(u_ref, m_ref, win_ref, wout_ref, a1_ref, a2_ref, d_ref, y_ref, *, n_steps):
    u = u_ref[0, 0]
    ub = u.astype(BF16)
    x = _dot(ub, win_ref[0])
    rows = lax.broadcasted_iota(jnp.int32, x.shape, 0)

    def shift_rows(v, s):
        return jnp.where(rows >= s, pltpu.roll(v, s, axis=0), 0.0)

    for step in range(n_steps):
        xs = shift_rows(x, 1 << step)
        x = x + a1_ref[0, step:step + 1, :] * xs + a2_ref[0, step:step + 1, :] * pltpu.roll(xs, SSM_STATE, axis=1)
    x_prev = shift_rows(x, 1)
    y = _dot(ub, m_ref[0]) + _dot(x_prev.astype(BF16), wout_ref[0]) + d_ref[0] * u
    y_ref[0, 0] = _gelu_tanh(y)


def _s5(u_g, m_op, w_in_op, w_out_op, a1, a2, d_t):
    bsz, groups, n_rows, width = u_g.shape
    n_steps = a1.shape[1]
    assert (1 << n_steps) == n_rows
    two_p = 2 * SSM_STATE
    body = functools.partial(_s5_body, n_steps=n_steps)
    per_group = lambda shape: pl.BlockSpec((1,) + shape, lambda b, g: (g, 0, 0))
    return pl.pallas_call(
        body,
        name="s5",
        grid=(bsz, groups),
        in_specs=[
            pl.BlockSpec((1, 1, n_rows, width), lambda b, g: (b, g, 0, 0)),
            per_group((width, width)),
            per_group((width, two_p)),
            per_group((two_p, width)),
            per_group((n_steps, two_p)),
            per_group((n_steps, two_p)),
            per_group((1, width)),
        ],
        out_specs=pl.BlockSpec((1, 1, n_rows, width), lambda b, g: (b, g, 0, 0)),
        out_shape=jax.ShapeDtypeStruct(u_g.shape, F32),
        compiler_params=pltpu.CompilerParams(dimension_semantics=("arbitrary", "arbitrary")),
    )(u_g, m_op, w_in_op, w_out_op, a1, a2, d_t)


def _ssm_operators(a_re, a_im, log_dt, b_re, b_im, c_re, c_im, d_skip, n_steps):
    t_len = SSM_CHUNK
    groups, n_state = a_re.shape
    hi = lax.Precision.HIGHEST
    dt = jnp.exp(log_dt)[:, None]
    mag = jnp.exp(a_re * dt)
    lb_re = mag * jnp.cos(a_im * dt)
    lb_im = mag * jnp.sin(a_im * dt)
    nr, ni = lb_re - 1.0, lb_im
    den = a_re * a_re + a_im * a_im
    cr = (nr * a_re + ni * a_im) / den
    ci = (ni * a_re - nr * a_im) / den
    bb_re = cr[..., None] * b_re - ci[..., None] * b_im
    bb_im = cr[..., None] * b_im + ci[..., None] * b_re

    def lam_pow(n):
        n = n.astype(F32)[:, None, None]
        m = jnp.exp(a_re * dt * n)
        return m * jnp.cos(a_im * dt * n), m * jnp.sin(a_im * dt * n)

    pw_re, pw_im = lam_pow(jnp.arange(t_len + 1))
    lbr = pw_re[..., None] * bb_re[None] - pw_im[..., None] * bb_im[None]
    lbi = pw_re[..., None] * bb_im[None] + pw_im[..., None] * bb_re[None]
    taps = (jnp.einsum('gcp,tgpd->tgcd', c_re, lbr[:t_len], precision=hi)
            - jnp.einsum('gcp,tgpd->tgcd', c_im, lbi[:t_len], precision=hi))
    s_idx = np.arange(t_len)[:, None]
    t_idx = np.arange(t_len)[None, :]
    lag = np.clip(t_idx - s_idx, 0, t_len - 1)
    toe = jnp.where((t_idx >= s_idx)[:, :, None, None, None], taps[lag], 0.0)
    m_op = toe.transpose(2, 0, 4, 1, 3).reshape(groups, t_len * SSM_GROUP, t_len * SSM_GROUP)
    w_in_re = lbr[:t_len][::-1].transpose(1, 0, 3, 2).reshape(groups, t_len * SSM_GROUP, n_state)
    w_in_im = lbi[:t_len][::-1].transpose(1, 0, 3, 2).reshape(groups, t_len * SSM_GROUP, n_state)
    w_in_op = jnp.concatenate([w_in_re, w_in_im], axis=-1)
    p_re = pw_re[1:, :, None, :]
    p_im = pw_im[1:, :, None, :]
    clr = c_re[None] * p_re - c_im[None] * p_im
    cli = c_re[None] * p_im + c_im[None] * p_re
    w_out_op = jnp.concatenate([clr.transpose(1, 3, 0, 2).reshape(groups, n_state, t_len * SSM_GROUP),
                                -cli.transpose(1, 3, 0, 2).reshape(groups, n_state, t_len * SSM_GROUP)], axis=1)
    pa_re, pa_im = lam_pow(t_len * (2 ** jnp.arange(n_steps)))
    a1 = jnp.concatenate([pa_re, pa_re], axis=-1).transpose(1, 0, 2)
    a2 = jnp.concatenate([-pa_im, pa_im], axis=-1).transpose(1, 0, 2)
    d_t = jnp.tile(d_skip, (1, t_len))[:, None, :]
    return m_op.astype(BF16), w_in_op.astype(BF16), w_out_op.astype(BF16), a1, a2, d_t


def _combine_body(o_ref, gy_ref, sg_ref, x_ref, wap_ref, wglu_ref, bglu_ref, wsp_ref, wout_ref, g2_ref,
                  h_ref, hn_ref):
    attn = _dot(o_ref[0], wap_ref[...])
    g = gy_ref[0]
    s = g * jax.nn.sigmoid(_dot(g.astype(BF16), wglu_ref[...]) + bglu_ref[...])
    ssm = _dot(s.astype(BF16), wsp_ref[...])
    sg = sg_ref[0]
    d_model = attn.shape[1]
    mix = sg[:, :d_model] * attn + sg[:, d_model:] * ssm
    h = x_ref[0] + _dot(mix.astype(BF16), wout_ref[...])
    h_ref[0] = h
    hn_ref[0] = (h * _rms_scale(h) * g2_ref[...]).astype(BF16)


def _combine(o, gy, sg, x, wap, wglu, bglu, wsp, wout, g2):
    bsz, seq, d_model = x.shape
    tm = TOKEN_TILE
    d_u = gy.shape[-1]
    tile = lambda width: pl.BlockSpec((1, tm, width), lambda b, i: (b, i, 0))
    return pl.pallas_call(
        _combine_body,
        name="combine",
        grid=(bsz, seq // tm),
        in_specs=[tile(o.shape[-1]), tile(d_u), tile(2 * d_model), tile(d_model),
                  _resident(wap.shape), _resident(wglu.shape), _resident(bglu.shape),
                  _resident(wsp.shape), _resident(wout.shape), _resident(g2.shape)],
        out_specs=[tile(d_model), tile(d_model)],
        out_shape=[jax.ShapeDtypeStruct(x.shape, F32), jax.ShapeDtypeStruct(x.shape, BF16)],
        compiler_params=pltpu.CompilerParams(
            dimension_semantics=("arbitrary", "arbitrary"), vmem_limit_bytes=VMEM_LIMIT),
    )(o, gy, sg, x, wap, wglu, bglu, wsp, wout, g2)


def _ffn_body(hn_ref, h_ref, wup_ref, cw_ref, cb_ref, wdn_ref, y_ref, carry_ref, *, d_ff):
    @pl.when(pl.program_id(1) == 0)
    def _():
        carry_ref[...] = jnp.zeros_like(carry_ref)

    hn = hn_ref[0]
    tm = hn.shape[0]

    def conv_cols(col0):
        cols = pl.ds(col0, FFN_CHUNK)
        up = _dot(hn, wup_ref[:, cols])
        cat = jnp.concatenate([carry_ref[:, cols], up], axis=0)
        carry_ref[:, cols] = up[tm - SUBLANES:, :]
        out = cb_ref[:, cols]
        for tap in range(CONV_WIDTH):
            lo = SUBLANES - (CONV_WIDTH - 1) + tap
            out = out + cat[lo:lo + tm, :] * cw_ref[tap:tap + 1, cols]
        return out

    acc = h_ref[0]
    for j in range(d_ff // FFN_CHUNK):
        gate = conv_cols(j * FFN_CHUNK)
        val = conv_cols(d_ff + j * FFN_CHUNK)
        act = (gate * jax.nn.sigmoid(gate) * val).astype(BF16)
        acc = acc + _dot(act, wdn_ref[j * FFN_CHUNK:(j + 1) * FFN_CHUNK, :])
    y_ref[0] = acc


def _convffn(hn, h, wup, cw, cb, wdn):
    bsz, seq, d_model = h.shape
    tm = TOKEN_TILE
    d_ff = wdn.shape[0]
    assert d_ff % FFN_CHUNK == 0 and wup.shape[1] == 2 * d_ff
    tile = pl.BlockSpec((1, tm, d_model), lambda b, i: (b, i, 0))
    body = functools.partial(_ffn_body, d_ff=d_ff)
    return pl.pallas_call(
        body,
        name="convffn",
        grid=(bsz, seq // tm),
        in_specs=[tile, tile, _resident(wup.shape), _resident(cw.shape), _resident(cb.shape),
                  _resident(wdn.shape)],
        out_specs=tile,
        out_shape=jax.ShapeDtypeStruct(h.shape, F32),
        scratch_shapes=[pltpu.VMEM((SUBLANES, 2 * d_ff), F32)],
        compiler_params=pltpu.CompilerParams(
            dimension_semantics=("arbitrary", "arbitrary"), vmem_limit_bytes=VMEM_LIMIT),
    )(hn, h, wup, cw, cb, wdn)


def kernel(x, norm1_gain, w_in, q_norm_gain, k_norm_gain, lambda_q1, lambda_k1, lambda_q2, lambda_k2,
           subln_gain, w_attn_proj, ssm_a_re, ssm_a_im, ssm_log_dt, ssm_b_re, ssm_b_im, ssm_c_re, ssm_c_im,
           ssm_d, w_glu, b_glu, w_ssm_proj, w_out, norm2_gain, w_up, conv_w, conv_b, w_down):
    bsz, seq, d_model = x.shape
    assert norm1_gain.shape[0] == 1, "single-layer trunk"
    assert seq % (TOKEN_TILE) == 0 and seq % SSM_CHUNK == 0
    d_qk = N_HEADS * HEAD_COLS

    perm, d_of_lane = _qk_column_permutation()
    cols = np.arange(w_in.shape[-1])
    cols[:d_qk] = perm
    cols[d_qk:2 * d_qk] = d_qk + perm
    w_in_b = w_in[0][:, cols].astype(BF16)
    lane_gain = lambda g: jnp.tile(g[d_of_lane], N_HEADS)[None, :]
    gq = lane_gain(q_norm_gain[0].astype(F32)) * (LOG2E / math.sqrt(HEAD_DIM))
    gk = lane_gain(k_norm_gain[0].astype(F32))

    pos = jnp.arange(seq, dtype=F32)
    inv_freq = 1.0 / (ROPE_THETA ** (jnp.arange(0, HEAD_DIM, 2, dtype=F32) / HEAD_DIM))
    ang = pos[:, None] * inv_freq[None, :]
    cos_t = jnp.tile(jnp.cos(ang), (1, HEAD_COLS // ROPE_HALF))
    sin_h = jnp.sin(ang)
    sin_t = jnp.concatenate([-sin_h, -sin_h, sin_h, sin_h], axis=1)

    comp = np.tile(_head_lane_component(), N_HEADS)
    head = np.repeat(np.arange(N_HEADS), HEAD_COLS)
    e_np = np.zeros((d_qk, LANES), np.float32)
    e_np[np.arange(d_qk), head * 2 + comp] = 1.0
    e_mat = jnp.asarray(e_np, BF16)
    et_mat = jnp.asarray(e_np.T, BF16)

    qT, k, vT, u, sg = _inproj(x, norm1_gain.astype(F32), w_in_b, gq, gk, cos_t, sin_t, e_mat, et_mat)

    lam = (jnp.exp(jnp.sum(lambda_q1[0].astype(F32) * lambda_k1[0].astype(F32)))
           - jnp.exp(jnp.sum(lambda_q2[0].astype(F32) * lambda_k2[0].astype(F32))) + LAM_INIT)
    o = _attention(lam.reshape(1, 1), qT, k, vT, subln_gain[0].astype(F32)[:, None])

    n_rows = seq // SSM_CHUNK
    n_steps = int(math.log2(n_rows))
    ops = _ssm_operators(ssm_a_re[0].astype(F32), ssm_a_im[0].astype(F32), ssm_log_dt[0].astype(F32),
                         ssm_b_re[0].astype(F32), ssm_b_im[0].astype(F32), ssm_c_re[0].astype(F32),
                         ssm_c_im[0].astype(F32), ssm_d[0].astype(F32), n_steps)
    u_g = (u.reshape(bsz, n_rows, SSM_CHUNK, SSM_GROUPS, SSM_GROUP)
           .transpose(0, 3, 1, 2, 4).reshape(bsz, SSM_GROUPS, n_rows, SSM_CHUNK * SSM_GROUP))
    gy_g = _s5(u_g, *ops)
    gy = (gy_g.reshape(bsz, SSM_GROUPS, n_rows, SSM_CHUNK, SSM_GROUP)
          .transpose(0, 2, 3, 1, 4).reshape(bsz, seq, SSM_GROUPS * SSM_GROUP))

    h, hn = _combine(o, gy, sg, x, w_attn_proj[0].astype(BF16), w_glu[0].astype(BF16),
                     b_glu.astype(F32), w_ssm_proj[0].astype(BF16), w_out[0].astype(BF16),
                     norm2_gain.astype(F32))
    return _convffn(hn, h, w_up[0].astype(BF16), conv_w[0].astype(F32), conv_b.astype(F32),
                    w_down[0].astype(BF16))
```

```python
import functools
import math

import numpy as np
import jax
import jax.numpy as jnp
from jax import lax
from jax.experimental import pallas as pl
from jax.experimental.pallas import tpu as pltpu

F32 = jnp.float32
BF16 = jnp.bfloat16

N_HEADS = 8
HEAD_DIM = 64
HEAD_COLS = 2 * HEAD_DIM
ROPE_HALF = HEAD_DIM // 2
V_DIM = 128
ROPE_THETA = 10000.0
SSM_GROUP = 16
SSM_GROUPS = 32
SSM_STATE = 64
SSM_CHUNK = 16
CONV_WIDTH = 3
EPS = 1e-6
LAM_INIT = 0.8 - 0.6 * math.exp(-0.3 * 0)

LANES = 128
SUBLANES = 8
TOKEN_TILE = 512
FFN_CHUNK = 256
VMEM_LIMIT = 56 * 1024 * 1024
MASK_VALUE = -1e30
SCORE_LOG2_LIMIT = 40.0
LOG2E = math.log2(math.e)


def _dot(a, b):
    return jnp.dot(a, b, preferred_element_type=F32)


def _rms_scale(x):
    return lax.rsqrt(jnp.mean(x * x, axis=-1, keepdims=True) + EPS)


def _resident(shape):
    return pl.BlockSpec(shape, lambda *_: (0,) * len(shape), pipeline_mode=pl.Buffered(1))


def _head_lane_component():
    j = np.arange(HEAD_COLS)
    return (j // ROPE_HALF) % 2


def _qk_column_permutation():
    j = np.arange(HEAD_COLS)
    comp = _head_lane_component()
    d = (j // HEAD_DIM) * ROPE_HALF + j % ROPE_HALF
    within = comp * HEAD_DIM + d
    return (np.arange(N_HEADS)[:, None] * HEAD_COLS + within[None, :]).reshape(-1), d


def _inproj_body(x_ref, g1_ref, w_ref, gq_ref, gk_ref, cos_ref, sin_ref, e_ref, et_ref,
                 qT_ref, k_ref, vT_ref, u_ref, sg_ref, *, d_qk, d_v, d_u):
    x = x_ref[0]
    xn = (x * _rms_scale(x) * g1_ref[...]).astype(BF16)
    cos = cos_ref[...]
    sin = sin_ref[...]

    def norm_rope(col0, gain_ref):
        t = _dot(xn, w_ref[:, col0:col0 + d_qk])
        ss = _dot((t * t).astype(BF16), e_ref[...])
        r = lax.rsqrt(ss * (1.0 / HEAD_DIM) + EPS)
        r_hi = r.astype(BF16)
        r_lo = (r - r_hi.astype(F32)).astype(BF16)
        rf = _dot(r_hi, et_ref[...]) + _dot(r_lo, et_ref[...])
        t = t * rf * gain_ref[...]
        heads = []
        for h in range(N_HEADS):
            th = t[:, h * HEAD_COLS:(h + 1) * HEAD_COLS]
            heads.append(th * cos + pltpu.roll(th, HEAD_DIM, axis=1) * sin)
        return jnp.concatenate(heads, axis=1)

    q = norm_rope(0, gq_ref)
    qT_ref[0, 0] = q.T.astype(BF16)
    k = norm_rope(d_qk, gk_ref)
    k_ref[0] = k.astype(BF16)
    c0 = 2 * d_qk
    v = _dot(xn, w_ref[:, c0:c0 + d_v])
    vT_ref[0, 0] = v.T.astype(BF16)
    c0 += d_v
    u_ref[0] = _dot(xn, w_ref[:, c0:c0 + d_u])
    c0 += d_u
    sg_ref[0] = jax.nn.sigmoid(_dot(xn, w_ref[:, c0:]))


def _inproj(x, g1, w_in_b, gq, gk, cos_t, sin_t, e_mat, et_mat):
    bsz, seq, d_model = x.shape
    tm = TOKEN_TILE
    nt = seq // tm
    d_qk = N_HEADS * HEAD_COLS
    d_v = N_HEADS * V_DIM
    d_u = SSM_GROUPS * SSM_GROUP
    d_g = 2 * d_model
    in_cols = w_in_b.shape[1]
    assert in_cols == 2 * d_qk + d_v + d_u + d_g and seq % tm == 0
    body = functools.partial(_inproj_body, d_qk=d_qk, d_v=d_v, d_u=d_u)
    return pl.pallas_call(
        body,
        name="inproj",
        grid=(bsz, nt),
        in_specs=[
            pl.BlockSpec((1, tm, d_model), lambda b, i: (b, i, 0)),
            _resident((1, d_model)),
            _resident((d_model, in_cols)),
            _resident((1, d_qk)),
            _resident((1, d_qk)),
            pl.BlockSpec((tm, LANES), lambda b, i: (i, 0)),
            pl.BlockSpec((tm, LANES), lambda b, i: (i, 0)),
            _resident((d_qk, LANES)),
            _resident((LANES, d_qk)),
        ],
        out_specs=[
            pl.BlockSpec((1, 1, d_qk, tm), lambda b, i: (b, i, 0, 0)),
            pl.BlockSpec((1, tm, d_qk), lambda b, i: (b, i, 0)),
            pl.BlockSpec((1, 1, d_v, tm), lambda b, i: (b, i, 0, 0)),
            pl.BlockSpec((1, tm, d_u), lambda b, i: (b, i, 0)),
            pl.BlockSpec((1, tm, d_g), lambda b, i: (b, i, 0)),
        ],
        out_shape=[
            jax.ShapeDtypeStruct((bsz, nt, d_qk, tm), BF16),
            jax.ShapeDtypeStruct((bsz, seq, d_qk), BF16),
            jax.ShapeDtypeStruct((bsz, nt, d_v, tm), BF16),
            jax.ShapeDtypeStruct((bsz, seq, d_u), F32),
            jax.ShapeDtypeStruct((bsz, seq, d_g), F32),
        ],
        compiler_params=pltpu.CompilerParams(
            dimension_semantics=("arbitrary", "arbitrary"), vmem_limit_bytes=VMEM_LIMIT),
    )(x, g1, w_in_b, gq, gk, cos_t, sin_t, e_mat, et_mat)


def _split_components(qT):
    row = lax.broadcasted_iota(jnp.int32, qT.shape, 0)
    is_c0 = ((row // ROPE_HALF) % 2) == 0
    zero = jnp.zeros_like(qT)
    return jnp.where(is_c0, qT, zero), jnp.where(is_c0, zero, qT)


def _causal(shape):
    return lax.broadcasted_iota(jnp.int32, shape, 0) <= lax.broadcasted_iota(jnp.int32, shape, 1)


def _attn_finish(lam_ref, gsub_ref, o_ref, acc0, acc1, l0, l1):
    lam = lam_ref[0, 0]
    a = acc0 * (1.0 / l0) - lam * (acc1 * (1.0 / l1))
    a = a * lax.rsqrt(jnp.mean(a * a, axis=0, keepdims=True) + EPS)
    a = a * gsub_ref[...] * (1.0 - LAM_INIT)
    o_ref[0] = a.T.astype(BF16)


def _attn_online_body(lam_ref, qT_ref, k_ref, vT_ref, gsub_ref, o_ref, acc0_ref, acc1_ref, *, blk):
    qi = pl.program_id(2)
    q_comp = _split_components(qT_ref[0, 0])
    accs = (acc0_ref, acc1_ref)
    acc0_ref[...] = jnp.zeros_like(acc0_ref)
    acc1_ref[...] = jnp.zeros_like(acc1_ref)

    def block(j, carry, masked):
        kj = k_ref[0, pl.ds(pl.multiple_of(j * blk, blk), blk), :]
        vj = vT_ref[0, j]
        out = []
        for c in range(2):
            m, l = carry[c]
            sT = _dot(kj, q_comp[c])
            if masked:
                sT = jnp.where(_causal(sT.shape), sT, MASK_VALUE)
            m_new = jnp.maximum(m, jnp.max(sT, axis=0, keepdims=True))
            alpha = jnp.exp2(m - m_new)
            p = jnp.exp2(sT - m_new)
            out.append((m_new, alpha * l + jnp.sum(p, axis=0, keepdims=True)))
            accs[c][...] = accs[c][...] * alpha + _dot(vj, p.astype(BF16))
        return tuple(out)

    init = ((jnp.full((1, blk), MASK_VALUE, F32), jnp.zeros((1, blk), F32)),) * 2
    carry = lax.fori_loop(0, qi, lambda j, c: block(j, c, False), init)
    (_, l0), (_, l1) = block(qi, carry, True)
    _attn_finish(lam_ref, gsub_ref, o_ref, acc0_ref[...], acc1_ref[...], l0, l1)


def _attn_bounded_body(lam_ref, qT_ref, k_ref, vT_ref, gsub_ref, o_ref,
                       acc0_ref, acc1_ref, sa_ref, sb_ref, *, blk):
    qi = pl.program_id(2)
    q_comp = _split_components(qT_ref[0, 0])
    accs = (acc0_ref, acc1_ref)
    acc0_ref[...] = jnp.zeros_like(acc0_ref)
    acc1_ref[...] = jnp.zeros_like(acc1_ref)

    def scores(j, s_ref):
        kj = k_ref[0, pl.ds(pl.multiple_of(j * blk, blk), blk), :]
        for c in range(2):
            s_ref[c] = _dot(kj, q_comp[c])

    def consume(j, s_ref, carry, masked):
        vj = vT_ref[0, j]
        out = []
        for c in range(2):
            p = jnp.exp2(s_ref[c])
            if masked:
                p = jnp.where(_causal(p.shape), p, 0.0)
            out.append(carry[c] + jnp.sum(p.reshape(blk // SUBLANES, SUBLANES, blk), axis=0))
            accs[c][...] += _dot(vj, p.astype(BF16))
        return tuple(out)

    def pair(t, carry):
        scores(2 * t + 1, sb_ref)
        carry = consume(2 * t, sa_ref, carry, False)
        scores(2 * t + 2, sa_ref)
        return consume(2 * t + 1, sb_ref, carry, False)

    def odd_tail(carry):
        scores(qi, sb_ref)
        carry = consume(qi - 1, sa_ref, carry, False)
        return consume(qi, sb_ref, carry, True)

    def even_tail(carry):
        return consume(qi, sa_ref, carry, True)

    scores(0, sa_ref)
    carry = lax.fori_loop(0, qi // 2, pair, (jnp.zeros((SUBLANES, blk), F32),) * 2)
    carry = lax.cond(qi % 2 == 1, odd_tail, even_tail, carry)
    l0, l1 = (jnp.sum(part, axis=0, keepdims=True) for part in carry)
    _attn_finish(lam_ref, gsub_ref, o_ref, acc0_ref[...], acc1_ref[...], l0, l1)


def _attention(lam, qT, k, vT, gsub, *, bounded):
    bsz, nblk, d_qk, blk = qT.shape
    seq = k.shape[1]
    scratch = [pltpu.VMEM((V_DIM, blk), F32), pltpu.VMEM((V_DIM, blk), F32)]
    if bounded:
        body = functools.partial(_attn_bounded_body, blk=blk)
        scratch += [pltpu.VMEM((2, blk, blk), F32), pltpu.VMEM((2, blk, blk), F32)]
    else:
        body = functools.partial(_attn_online_body, blk=blk)
    return pl.pallas_call(
        body,
        name="attention_bounded" if bounded else "attention_online",
        grid=(bsz, N_HEADS, nblk),
        in_specs=[
            pl.BlockSpec(memory_space=pltpu.SMEM),
            pl.BlockSpec((1, 1, HEAD_COLS, blk), lambda b, h, i: (b, i, h, 0)),
            pl.BlockSpec((1, seq, HEAD_COLS), lambda b, h, i: (b, 0, h)),
            pl.BlockSpec((1, nblk, V_DIM, blk), lambda b, h, i: (b, 0, h, 0)),
            pl.BlockSpec((V_DIM, 1), lambda b, h, i: (0, 0)),
        ],
        out_specs=pl.BlockSpec((1, blk, V_DIM), lambda b, h, i: (b, i, h)),
        out_shape=jax.ShapeDtypeStruct((bsz, seq, N_HEADS * V_DIM), BF16),
        scratch_shapes=scratch,
        compiler_params=pltpu.CompilerParams(
            dimension_semantics=("arbitrary", "arbitrary", "arbitrary"), vmem_limit_bytes=VMEM_LIMIT),
    )(lam, qT, k, vT, gsub)


def _gelu_tanh(y):
    return 0.5 * y * (1.0 + jnp.tanh(math.sqrt(2.0 / math.pi) * (y + 0.044715 * (y * y * y))))


def _s5_body(u_ref, m_ref, win_ref, wout_ref, a1_ref, a2_ref, d_ref, y_ref, *, n_steps):
    u = u_ref[0, 0]
    ub = u.astype(BF16)
    x = _dot(ub, win_ref[0])
    rows = lax.broadcasted_iota(jnp.int32, x.shape, 0)

    def shift_rows(v, s):
        return jnp.where(rows >= s, pltpu.roll(v, s, axis=0), 0.0)

    for step in range(n_steps):
        xs = shift_rows(x, 1 << step)
        x = x + a1_ref[0, step:step + 1, :] * xs + a2_ref[0, step:step + 1, :] * pltpu.roll(xs, SSM_STATE, axis=1)
    x_prev = shift_rows(x, 1)
    y = _dot(ub, m_ref[0]) + _dot(x_prev.astype(BF16), wout_ref[0]) + d_ref[0] * u
    y_ref[0, 0] = _gelu_tanh(y)


def _s5(u_g, m_op, w_in_op, w_out_op, a1, a2, d_t):
    bsz, groups, n_rows, width = u_g.shape
    n_steps = a1.shape[1]
    assert (1 << n_steps) == n_rows
    two_p = 2 * SSM_STATE
    body = functools.partial(_s5_body, n_steps=n_steps)
    per_group = lambda shape: pl.BlockSpec((1,) + shape, lambda b, g: (g, 0, 0))
    return pl.pallas_call(
        body,
        name="s5",
        grid=(bsz, groups),
        in_specs=[
            pl.BlockSpec((1, 1, n_rows, width), lambda b, g: (b, g, 0, 0)),
            per_group((width, width)),
            per_group((width, two_p)),
            per_group((two_p, width)),
            per_group((n_steps, two_p)),
            per_group((n_steps, two_p)),
            per_group((1, width)),
        ],
        out_specs=pl.BlockSpec((1, 1, n_rows, width), lambda b, g: (b, g, 0, 0)),
        out_shape=jax.ShapeDtypeStruct(u_g.shape, F32),
        compiler_params=pltpu.CompilerParams(dimension_semantics=("arbitrary", "arbitrary")),
    )(u_g, m_op, w_in_op, w_out_op, a1, a2, d_t)


def _ssm_operators(a_re, a_im, log_dt, b_re, b_im, c_re, c_im, d_skip, n_steps):
    t_len = SSM_CHUNK
    groups, n_state = a_re.shape
    hi = lax.Precision.HIGHEST
    dt = jnp.exp(log_dt)[:, None]
    mag = jnp.exp(a_re * dt)
    lb_re = mag * jnp.cos(a_im * dt)
    lb_im = mag * jnp.sin(a_im * dt)
    nr, ni = lb_re - 1.0, lb_im
    den = a_re * a_re + a_im * a_im
    cr = (nr * a_re + ni * a_im) / den
    ci = (ni * a_re - nr * a_im) / den
    bb_re = cr[..., None] * b_re - ci[..., None] * b_im
    bb_im = cr[..., None] * b_im + ci[..., None] * b_re

    def lam_pow(n):
        n = n.astype(F32)[:, None, None]
        m = jnp.exp(a_re * dt * n)
        return m * jnp.cos(a_im * dt * n), m * jnp.sin(a_im * dt * n)

    pw_re, pw_im = lam_pow(jnp.arange(t_len + 1))
    lbr = pw_re[..., None] * bb_re[None] - pw_im[..., None] * bb_im[None]
    lbi = pw_re[..., None] * bb_im[None] + pw_im[..., None] * bb_re[None]
    taps = (jnp.einsum('gcp,tgpd->tgcd', c_re, lbr[:t_len], precision=hi)
            - jnp.einsum('gcp,tgpd->tgcd', c_im, lbi[:t_len], precision=hi))
    s_idx = np.arange(t_len)[:, None]
    t_idx = np.arange(t_len)[None, :]
    lag = np.clip(t_idx - s_idx, 0, t_len - 1)
    toe = jnp.where((t_idx >= s_idx)[:, :, None, None, None], taps[lag], 0.0)
    m_op = toe.transpose(2, 0, 4, 1, 3).reshape(groups, t_len * SSM_GROUP, t_len * SSM_GROUP)
    w_in_re = lbr[:t_len][::-1].transpose(1, 0, 3, 2).reshape(groups, t_len * SSM_GROUP, n_state)
    w_in_im = lbi[:t_len][::-1].transpose(1, 0, 3, 2).reshape(groups, t_len * SSM_GROUP, n_state)
    w_in_op = jnp.concatenate([w_in_re, w_in_im], axis=-1)
    p_re = pw_re[1:, :, None, :]
    p_im = pw_im[1:, :, None, :]
    clr = c_re[None] * p_re - c_im[None] * p_im
    cli = c_re[None] * p_im + c_im[None] * p_re
    w_out_op = jnp.concatenate([clr.transpose(1, 3, 0, 2).reshape(groups, n_state, t_len * SSM_GROUP),
                                -cli.transpose(1, 3, 0, 2).reshape(groups, n_state, t_len * SSM_GROUP)], axis=1)
    pa_re, pa_im = lam_pow(t_len * (2 ** jnp.arange(n_steps)))
    a1 = jnp.concatenate([pa_re, pa_re], axis=-1).transpose(1, 0, 2)
    a2 = jnp.concatenate([-pa_im, pa_im], axis=-1).transpose(1, 0, 2)
    d_t = jnp.tile(d_skip, (1, t_len))[:, None, :]
    return m_op.astype(BF16), w_in_op.astype(BF16), w_out_op.astype(BF16), a1, a2, d_t


def _combine_body(o_ref, gy_ref, sg_ref, x_ref, wap_ref, wglu_ref, bglu_ref, wsp_ref, wout_ref, g2_ref,
                  h_ref, hn_ref):
    attn = _dot(o_ref[0], wap_ref[...])
    g = gy_ref[0]
    s = g * jax.nn.sigmoid(_dot(g.astype(BF16), wglu_ref[...]) + bglu_ref[...])
    ssm = _dot(s.astype(BF16), wsp_ref[...])
    sg = sg_ref[0]
    d_model = attn.shape[1]
    mix = sg[:, :d_model] * attn + sg[:, d_model:] * ssm
    h = x_ref[0] + _dot(mix.astype(BF16), wout_ref[...])
    h_ref[0] = h
    hn_ref[0] = (h * _rms_scale(h) * g2_ref[...]).astype(BF16)


def _combine(o, gy, sg, x, wap, wglu, bglu, wsp, wout, g2):
    bsz, seq, d_model = x.shape
    tm = TOKEN_TILE
    d_u = gy.shape[-1]
    tile = lambda width: pl.BlockSpec((1, tm, width), lambda b, i: (b, i, 0))
    return pl.pallas_call(
        _combine_body,
        name="combine",
        grid=(bsz, seq // tm),
        in_specs=[tile(o.shape[-1]), tile(d_u), tile(2 * d_model), tile(d_model),
                  _resident(wap.shape), _resident(wglu.shape), _resident(bglu.shape),
                  _resident(wsp.shape), _resident(wout.shape), _resident(g2.shape)],
        out_specs=[tile(d_model), tile(d_model)],
        out_shape=[jax.ShapeDtypeStruct(x.shape, F32), jax.ShapeDtypeStruct(x.shape, BF16)],
        compiler_params=pltpu.CompilerParams(
            dimension_semantics=("arbitrary", "arbitrary"), vmem_limit_bytes=VMEM_LIMIT),
    )(o, gy, sg, x, wap, wglu, bglu, wsp, wout, g2)


def _ffn_body(hn_ref, h_ref, wup_ref, cw_ref, cb_ref, wdn_ref, y_ref, carry_ref, *, d_ff):
    @pl.when(pl.program_id(1) == 0)
    def _():
        carry_ref[...] = jnp.zeros_like(carry_ref)

    hn = hn_ref[0]
    tm = hn.shape[0]

    def conv_cols(col0):
        cols = pl.ds(col0, FFN_CHUNK)
        up = _dot(hn, wup_ref[:, cols])
        cat = jnp.concatenate([carry_ref[:, cols], up], axis=0)
        carry_ref[:, cols] = up[tm - SUBLANES:, :]
        out = cb_ref[:, cols]
        for tap in range(CONV_WIDTH):
            lo = SUBLANES - (CONV_WIDTH - 1) + tap
            out = out + cat[lo:lo + tm, :] * cw_ref[tap:tap + 1, cols]
        return out

    acc = h_ref[0]
    for j in range(d_ff // FFN_CHUNK):
        gate = conv_cols(j * FFN_CHUNK)
        val = conv_cols(d_ff + j * FFN_CHUNK)
        act = (gate * jax.nn.sigmoid(gate) * val).astype(BF16)
        acc = acc + _dot(act, wdn_ref[j * FFN_CHUNK:(j + 1) * FFN_CHUNK, :])
    y_ref[0] = acc


def _convffn(hn, h, wup, cw, cb, wdn):
    bsz, seq, d_model = h.shape
    tm = TOKEN_TILE
    d_ff = wdn.shape[0]
    assert d_ff % FFN_CHUNK == 0 and wup.shape[1] == 2 * d_ff
    tile = pl.BlockSpec((1, tm, d_model), lambda b, i: (b, i, 0))
    body = functools.partial(_ffn_body, d_ff=d_ff)
    return pl.pallas_call(
        body,
        name="convffn",
        grid=(bsz, seq // tm),
        in_specs=[tile, tile, _resident(wup.shape), _resident(cw.shape), _resident(cb.shape),
                  _resident(wdn.shape)],
        out_specs=tile,
        out_shape=jax.ShapeDtypeStruct(h.shape, F32),
        scratch_shapes=[pltpu.VMEM((SUBLANES, 2 * d_ff), F32)],
        compiler_params=pltpu.CompilerParams(
            dimension_semantics=("arbitrary", "arbitrary"), vmem_limit_bytes=VMEM_LIMIT),
    )(hn, h, wup, cw, cb, wdn)


def kernel(x, norm1_gain, w_in, q_norm_gain, k_norm_gain, lambda_q1, lambda_k1, lambda_q2, lambda_k2,
           subln_gain, w_attn_proj, ssm_a_re, ssm_a_im, ssm_log_dt, ssm_b_re, ssm_b_im, ssm_c_re, ssm_c_im,
           ssm_d, w_glu, b_glu, w_ssm_proj, w_out, norm2_gain, w_up, conv_w, conv_b, w_down):
    bsz, seq, d_model = x.shape
    assert norm1_gain.shape[0] == 1, "single-layer trunk"
    assert seq % (TOKEN_TILE) == 0 and seq % SSM_CHUNK == 0
    d_qk = N_HEADS * HEAD_COLS

    perm, d_of_lane = _qk_column_permutation()
    cols = np.arange(w_in.shape[-1])
    cols[:d_qk] = perm
    cols[d_qk:2 * d_qk] = d_qk + perm
    w_in_b = w_in[0][:, cols].astype(BF16)
    lane_gain = lambda g: jnp.tile(g[d_of_lane], N_HEADS)[None, :]
    gq = lane_gain(q_norm_gain[0].astype(F32)) * (LOG2E / math.sqrt(HEAD_DIM))
    gk = lane_gain(k_norm_gain[0].astype(F32))

    pos = jnp.arange(seq, dtype=F32)
    inv_freq = 1.0 / (ROPE_THETA ** (jnp.arange(0, HEAD_DIM, 2, dtype=F32) / HEAD_DIM))
    ang = pos[:, None] * inv_freq[None, :]
    cos_t = jnp.tile(jnp.cos(ang), (1, HEAD_COLS // ROPE_HALF))
    sin_h = jnp.sin(ang)
    sin_t = jnp.concatenate([-sin_h, -sin_h, sin_h, sin_h], axis=1)

    comp = np.tile(_head_lane_component(), N_HEADS)
    head = np.repeat(np.arange(N_HEADS), HEAD_COLS)
    e_np = np.zeros((d_qk, LANES), np.float32)
    e_np[np.arange(d_qk), head * 2 + comp] = 1.0
    e_mat = jnp.asarray(e_np, BF16)
    et_mat = jnp.asarray(e_np.T, BF16)

    qT, k, vT, u, sg = _inproj(x, norm1_gain.astype(F32), w_in_b, gq, gk, cos_t, sin_t, e_mat, et_mat)

    lam = (jnp.exp(jnp.sum(lambda_q1[0].astype(F32) * lambda_k1[0].astype(F32)))
           - jnp.exp(jnp.sum(lambda_q2[0].astype(F32) * lambda_k2[0].astype(F32))) + LAM_INIT)
    score_bound = HEAD_DIM * jnp.max(jnp.abs(gq)) * jnp.max(jnp.abs(gk))
    o = lax.cond(score_bound <= SCORE_LOG2_LIMIT,
                 functools.partial(_attention, bounded=True),
                 functools.partial(_attention, bounded=False),
                 lam.reshape(1, 1), qT, k, vT, subln_gain[0].astype(F32)[:, None])

    n_rows = seq // SSM_CHUNK
    n_steps = int(math.log2(n_rows))
    ops = _ssm_operators(ssm_a_re[0].astype(F32), ssm_a_im[0].astype(F32), ssm_log_dt[0].astype(F32),
                         ssm_b_re[0].astype(F32), ssm_b_im[0].astype(F32), ssm_c_re[0].astype(F32),
                         ssm_c_im[0].astype(F32), ssm_d[0].astype(F32), n_steps)
    u_g = (u.reshape(bsz, n_rows, SSM_CHUNK, SSM_GROUPS, SSM_GROUP)
           .transpose(0, 3, 1, 2, 4).reshape(bsz, SSM_GROUPS, n_rows, SSM_CHUNK * SSM_GROUP))
    gy_g = _s5(u_g, *ops)
    gy = (gy_g.reshape(bsz, SSM_GROUPS, n_rows, SSM_CHUNK, SSM_GROUP)
          .transpose(0, 2, 3, 1, 4).reshape(bsz, seq, SSM_GROUPS * SSM_GROUP))

    h, hn = _combine(o, gy, sg, x, w_attn_proj[0].astype(BF16), w_glu[0].astype(BF16),
                     b_glu.astype(F32), w_ssm_proj[0].astype(BF16), w_out[0].astype(BF16),
                     norm2_gain.astype(F32))
    return _convffn(hn, h, w_up[0].astype(BF16), conv_w[0].astype(F32), conv_b.astype(F32),
                    w_down[0].astype(BF16))
```

```python
import functools
import math

import numpy as np
import jax
import jax.numpy as jnp
from jax import lax
from jax.experimental import pallas as pl
from jax.experimental.pallas import tpu as pltpu

F32 = jnp.float32
BF16 = jnp.bfloat16

N_HEADS = 8
HEAD_DIM = 64
HEAD_COLS = 2 * HEAD_DIM
ROPE_HALF = HEAD_DIM // 2
V_DIM = 128
ROPE_THETA = 10000.0
SSM_GROUP = 16
SSM_GROUPS = 32
SSM_STATE = 64
SSM_CHUNK = 16
CONV_WIDTH = 3
EPS = 1e-6
LAM_INIT = 0.8 - 0.6 * math.exp(-0.3 * 0)

LANES = 128
SUBLANES = 8
TOKEN_TILE = 512
FFN_CHUNK = 256
ATTN_UNROLL = 4
SSM_TILE = 4096
VMEM_LIMIT = 56 * 1024 * 1024
MASK_VALUE = -1e30
SCORE_LOG2_LIMIT = 40.0
LOG2E = math.log2(math.e)


def _dot(a, b):
    return jnp.dot(a, b, preferred_element_type=F32)


def _rms_scale(x):
    return lax.rsqrt(jnp.mean(x * x, axis=-1, keepdims=True) + EPS)


def _resident(shape):
    return pl.BlockSpec(shape, lambda *_: (0,) * len(shape), pipeline_mode=pl.Buffered(1))


def _head_lane_component():
    j = np.arange(HEAD_COLS)
    return (j // ROPE_HALF) % 2


def _qk_column_permutation():
    j = np.arange(HEAD_COLS)
    comp = _head_lane_component()
    d = (j // HEAD_DIM) * ROPE_HALF + j % ROPE_HALF
    within = comp * HEAD_DIM + d
    return (np.arange(N_HEADS)[:, None] * HEAD_COLS + within[None, :]).reshape(-1), d


def _inproj_body(x_ref, g1_ref, w_ref, gq_ref, gk_ref, cos_ref, sin_ref, e_ref, et_ref,
                 qT_ref, k_ref, vT_ref, u_ref, sg_ref, *, d_qk, d_v, d_u):
    x = x_ref[0]
    xn = (x * _rms_scale(x) * g1_ref[...]).astype(BF16)
    cos = cos_ref[...]
    sin = sin_ref[...]

    def norm_rope(col0, gain_ref):
        t = _dot(xn, w_ref[:, col0:col0 + d_qk])
        ss = _dot((t * t).astype(BF16), e_ref[...])
        r = lax.rsqrt(ss * (1.0 / HEAD_DIM) + EPS)
        r_hi = r.astype(BF16)
        r_lo = (r - r_hi.astype(F32)).astype(BF16)
        rf = _dot(r_hi, et_ref[...]) + _dot(r_lo, et_ref[...])
        t = t * rf * gain_ref[...]
        heads = []
        for h in range(N_HEADS):
            th = t[:, h * HEAD_COLS:(h + 1) * HEAD_COLS]
            heads.append(th * cos + pltpu.roll(th, HEAD_DIM, axis=1) * sin)
        return jnp.concatenate(heads, axis=1)

    q = norm_rope(0, gq_ref)
    qT_ref[0, 0] = q.T.astype(BF16)
    k = norm_rope(d_qk, gk_ref)
    k_ref[0] = k.astype(BF16)
    c0 = 2 * d_qk
    v = _dot(xn, w_ref[:, c0:c0 + d_v])
    vT_ref[0, 0] = v.T.astype(BF16)
    c0 += d_v
    u_ref[0] = _dot(xn, w_ref[:, c0:c0 + d_u])
    c0 += d_u
    sg_ref[0] = jax.nn.sigmoid(_dot(xn, w_ref[:, c0:]))


def _inproj(x, g1, w_in_b, gq, gk, cos_t, sin_t, e_mat, et_mat):
    bsz, seq, d_model = x.shape
    tm = TOKEN_TILE
    nt = seq // tm
    d_qk = N_HEADS * HEAD_COLS
    d_v = N_HEADS * V_DIM
    d_u = SSM_GROUPS * SSM_GROUP
    d_g = 2 * d_model
    in_cols = w_in_b.shape[1]
    assert in_cols == 2 * d_qk + d_v + d_u + d_g and seq % tm == 0
    body = functools.partial(_inproj_body, d_qk=d_qk, d_v=d_v, d_u=d_u)
    return pl.pallas_call(
        body,
        name="inproj",
        grid=(bsz, nt),
        in_specs=[
            pl.BlockSpec((1, tm, d_model), lambda b, i: (b, i, 0)),
            _resident((1, d_model)),
            _resident((d_model, in_cols)),
            _resident((1, d_qk)),
            _resident((1, d_qk)),
            pl.BlockSpec((tm, LANES), lambda b, i: (i, 0)),
            pl.BlockSpec((tm, LANES), lambda b, i: (i, 0)),
            _resident((d_qk, LANES)),
            _resident((LANES, d_qk)),
        ],
        out_specs=[
            pl.BlockSpec((1, 1, d_qk, tm), lambda b, i: (b, i, 0, 0)),
            pl.BlockSpec((1, tm, d_qk), lambda b, i: (b, i, 0)),
            pl.BlockSpec((1, 1, d_v, tm), lambda b, i: (b, i, 0, 0)),
            pl.BlockSpec((1, tm, d_u), lambda b, i: (b, i, 0)),
            pl.BlockSpec((1, tm, d_g), lambda b, i: (b, i, 0)),
        ],
        out_shape=[
            jax.ShapeDtypeStruct((bsz, nt, d_qk, tm), BF16),
            jax.ShapeDtypeStruct((bsz, seq, d_qk), BF16),
            jax.ShapeDtypeStruct((bsz, nt, d_v, tm), BF16),
            jax.ShapeDtypeStruct((bsz, seq, d_u), F32),
            jax.ShapeDtypeStruct((bsz, seq, d_g), F32),
        ],
        compiler_params=pltpu.CompilerParams(
            dimension_semantics=("arbitrary", "arbitrary"), vmem_limit_bytes=VMEM_LIMIT),
    )(x, g1, w_in_b, gq, gk, cos_t, sin_t, e_mat, et_mat)


def _split_components(qT):
    row = lax.broadcasted_iota(jnp.int32, qT.shape, 0)
    is_c0 = ((row // ROPE_HALF) % 2) == 0
    zero = jnp.zeros_like(qT)
    return jnp.where(is_c0, qT, zero), jnp.where(is_c0, zero, qT)


def _causal(shape):
    return lax.broadcasted_iota(jnp.int32, shape, 0) <= lax.broadcasted_iota(jnp.int32, shape, 1)


def _attn_finish(lam_ref, gsub_ref, o_ref, acc0, acc1, l0, l1):
    lam = lam_ref[0, 0]
    a = acc0 * (1.0 / l0) - lam * (acc1 * (1.0 / l1))
    a = a * lax.rsqrt(jnp.mean(a * a, axis=0, keepdims=True) + EPS)
    a = a * gsub_ref[...] * (1.0 - LAM_INIT)
    o_ref[0] = a.T.astype(BF16)


def _attn_online_body(lam_ref, qT_ref, k_ref, vT_ref, gsub_ref, o_ref, acc0_ref, acc1_ref, *, blk):
    qi = pl.program_id(2)
    q_comp = _split_components(qT_ref[0, 0])
    accs = (acc0_ref, acc1_ref)
    acc0_ref[...] = jnp.zeros_like(acc0_ref)
    acc1_ref[...] = jnp.zeros_like(acc1_ref)

    def block(j, carry, masked):
        kj = k_ref[0, pl.ds(pl.multiple_of(j * blk, blk), blk), :]
        vj = vT_ref[0, j]
        out = []
        for c in range(2):
            m, l = carry[c]
            sT = _dot(kj, q_comp[c])
            if masked:
                sT = jnp.where(_causal(sT.shape), sT, MASK_VALUE)
            m_new = jnp.maximum(m, jnp.max(sT, axis=0, keepdims=True))
            alpha = jnp.exp2(m - m_new)
            p = jnp.exp2(sT - m_new)
            out.append((m_new, alpha * l + jnp.sum(p, axis=0, keepdims=True)))
            accs[c][...] = accs[c][...] * alpha + _dot(vj, p.astype(BF16))
        return tuple(out)

    init = ((jnp.full((1, blk), MASK_VALUE, F32), jnp.zeros((1, blk), F32)),) * 2
    carry = lax.fori_loop(0, qi, lambda j, c: block(j, c, False), init)
    (_, l0), (_, l1) = block(qi, carry, True)
    _attn_finish(lam_ref, gsub_ref, o_ref, acc0_ref[...], acc1_ref[...], l0, l1)


def _attn_bounded_body(lam_ref, qT_ref, k_ref, vT_ref, gsub_ref, o_ref,
                       acc0_ref, acc1_ref, sa_ref, sb_ref, *, blk):
    qi = pl.program_id(2)
    q_comp = _split_components(qT_ref[0, 0])
    accs = (acc0_ref, acc1_ref)
    acc0_ref[...] = jnp.zeros_like(acc0_ref)
    acc1_ref[...] = jnp.zeros_like(acc1_ref)

    def scores(j, s_ref):
        kj = k_ref[0, pl.ds(pl.multiple_of(j * blk, blk), blk), :]
        for c in range(2):
            s_ref[c] = _dot(kj, q_comp[c])

    def consume(j, s_ref, carry, masked):
        vj = vT_ref[0, j]
        out = []
        for c in range(2):
            p = jnp.exp2(s_ref[c])
            if masked:
                p = jnp.where(_causal(p.shape), p, 0.0)
            out.append(carry[c] + jnp.sum(p.reshape(blk // SUBLANES, SUBLANES, blk), axis=0))
            accs[c][...] += _dot(vj, p.astype(BF16))
        return tuple(out)

    bufs = (sa_ref, sb_ref)

    def run(first, count, carry, diagonal_last):
        for n in range(count):
            last = n == count - 1
            if not (last and diagonal_last):
                scores(first + n + 1, bufs[(n + 1) % 2])
            carry = consume(first + n, bufs[n % 2], carry, last and diagonal_last)
        return carry

    scores(0, sa_ref)
    trips = qi // ATTN_UNROLL
    carry = lax.fori_loop(0, trips, lambda t, c: run(ATTN_UNROLL * t, ATTN_UNROLL, c, False),
                          (jnp.zeros((SUBLANES, blk), F32),) * 2)
    first = trips * ATTN_UNROLL
    tails = [functools.partial(run, first, rem + 1, diagonal_last=True) for rem in range(ATTN_UNROLL)]
    carry = lax.switch(qi - first, tails, carry)
    l0, l1 = (jnp.sum(part, axis=0, keepdims=True) for part in carry)
    _attn_finish(lam_ref, gsub_ref, o_ref, acc0_ref[...], acc1_ref[...], l0, l1)


def _attention(lam, qT, k, vT, gsub, *, bounded):
    bsz, nblk, d_qk, blk = qT.shape
    seq = k.shape[1]
    scratch = [pltpu.VMEM((V_DIM, blk), F32), pltpu.VMEM((V_DIM, blk), F32)]
    if bounded:
        body = functools.partial(_attn_bounded_body, blk=blk)
        scratch += [pltpu.VMEM((2, blk, blk), F32), pltpu.VMEM((2, blk, blk), F32)]
    else:
        body = functools.partial(_attn_online_body, blk=blk)
    return pl.pallas_call(
        body,
        name="attention_bounded" if bounded else "attention_online",
        grid=(bsz, N_HEADS, nblk),
        in_specs=[
            pl.BlockSpec(memory_space=pltpu.SMEM),
            pl.BlockSpec((1, 1, HEAD_COLS, blk), lambda b, h, i: (b, i, h, 0)),
            pl.BlockSpec((1, seq, HEAD_COLS), lambda b, h, i: (b, 0, h)),
            pl.BlockSpec((1, nblk, V_DIM, blk), lambda b, h, i: (b, 0, h, 0)),
            pl.BlockSpec((V_DIM, 1), lambda b, h, i: (0, 0)),
        ],
        out_specs=pl.BlockSpec((1, blk, V_DIM), lambda b, h, i: (b, i, h)),
        out_shape=jax.ShapeDtypeStruct((bsz, seq, N_HEADS * V_DIM), BF16),
        scratch_shapes=scratch,
        compiler_params=pltpu.CompilerParams(
            dimension_semantics=("arbitrary", "arbitrary", "arbitrary"), vmem_limit_bytes=VMEM_LIMIT),
    )(lam, qT, k, vT, gsub)


def _gelu_tanh(y):
    return 0.5 * y * (1.0 + jnp.tanh(math.sqrt(2.0 / math.pi) * (y + 0.044715 * (y * y * y))))


def _s5_body(u_ref, m_ref, win_ref, wout_ref, a1_ref, a2_ref, d_ref, y_ref, carry_ref, *, n_steps):
    @pl.when(pl.program_id(2) == 0)
    def _():
        carry_ref[...] = jnp.zeros_like(carry_ref)

    n_rows = u_ref.shape[1] // SSM_CHUNK
    slabs = [u_ref[0, pl.ds(s, n_rows, stride=SSM_CHUNK), :] for s in range(SSM_CHUNK)]
    ucat = jnp.concatenate([slab.astype(BF16) for slab in slabs], axis=1)
    x = _dot(ucat, win_ref[0])
    rows = lax.broadcasted_iota(jnp.int32, (n_rows, LANES), 0)
    x_prev = []
    for i in range(LANES // SSM_GROUP):
        lanes = slice(i * LANES, (i + 1) * LANES)
        a1 = a1_ref[0, :, lanes]
        a2 = a2_ref[0, :, lanes]
        x_in = carry_ref[:, lanes]

        def times_a(v, k):
            return a1[k:k + 1] * v + a2[k:k + 1] * pltpu.roll(v, SSM_STATE, axis=1)

        xi = x[:, lanes] + jnp.where(rows == 0, times_a(x_in, 0)[0:1], 0.0)
        for k in range(n_steps):
            xi = xi + times_a(jnp.where(rows >= (1 << k), pltpu.roll(xi, 1 << k, axis=0), 0.0), k)
        x_prev.append(jnp.where(rows == 0, x_in[0:1], pltpu.roll(xi, 1, axis=0)))
        carry_ref[:, lanes] = jnp.broadcast_to(xi[n_rows - 1:n_rows], (SUBLANES, LANES))
    x_prev = jnp.concatenate(x_prev, axis=1).astype(BF16)
    y = _dot(ucat, m_ref[0]) + _dot(x_prev, wout_ref[0])
    for t in range(SSM_CHUNK):
        yt = y[:, t * LANES:(t + 1) * LANES] + d_ref[0] * slabs[t]
        y_ref[0, pl.ds(t, n_rows, stride=SSM_CHUNK), :] = _gelu_tanh(yt)


def _s5(u, m_op, w_in_op, w_out_op, a1, a2, d_t):
    bsz, seq, width = u.shape
    n_blocks = width // LANES
    tile = min(SSM_TILE, seq)
    n_steps = a1.shape[1]
    assert (1 << n_steps) == tile // SSM_CHUNK and seq % tile == 0
    body = functools.partial(_s5_body, n_steps=n_steps)
    per_block = lambda a: pl.BlockSpec((1,) + a.shape[1:], lambda v, b, i: (v, 0, 0))
    tokens = pl.BlockSpec((1, tile, LANES), lambda v, b, i: (b, i, v))
    return pl.pallas_call(
        body,
        name="s5",
        grid=(n_blocks, bsz, seq // tile),
        in_specs=[tokens, per_block(m_op), per_block(w_in_op), per_block(w_out_op),
                  per_block(a1), per_block(a2), per_block(d_t)],
        out_specs=tokens,
        out_shape=jax.ShapeDtypeStruct(u.shape, F32),
        scratch_shapes=[pltpu.VMEM((SUBLANES, w_in_op.shape[-1]), F32)],
        compiler_params=pltpu.CompilerParams(
            dimension_semantics=("arbitrary", "arbitrary", "arbitrary"), vmem_limit_bytes=VMEM_LIMIT),
    )(u, m_op, w_in_op, w_out_op, a1, a2, d_t)


def _ssm_operators(a_re, a_im, log_dt, b_re, b_im, c_re, c_im, d_skip, n_steps):
    t_len = SSM_CHUNK
    groups, n_state = a_re.shape
    per = LANES // SSM_GROUP
    nv = groups // per
    hi = lax.Precision.HIGHEST
    dt = jnp.exp(log_dt)[:, None]
    mag = jnp.exp(a_re * dt)
    lb_re = mag * jnp.cos(a_im * dt)
    lb_im = mag * jnp.sin(a_im * dt)
    nr, ni = lb_re - 1.0, lb_im
    den = a_re * a_re + a_im * a_im
    cr = (nr * a_re + ni * a_im) / den
    ci = (ni * a_re - nr * a_im) / den
    bb_re = cr[..., None] * b_re - ci[..., None] * b_im
    bb_im = cr[..., None] * b_im + ci[..., None] * b_re

    def lam_pow(n):
        n = n.astype(F32)[:, None, None]
        m = jnp.exp(a_re * dt * n)
        return m * jnp.cos(a_im * dt * n), m * jnp.sin(a_im * dt * n)

    pw_re, pw_im = lam_pow(jnp.arange(t_len + 1))
    lbr = pw_re[..., None] * bb_re[None] - pw_im[..., None] * bb_im[None]
    lbi = pw_re[..., None] * bb_im[None] + pw_im[..., None] * bb_re[None]
    taps = (jnp.einsum('gcp,tgpd->tgcd', c_re, lbr[:t_len], precision=hi)
            - jnp.einsum('gcp,tgpd->tgcd', c_im, lbi[:t_len], precision=hi))
    s_idx = np.arange(t_len)[:, None]
    t_idx = np.arange(t_len)[None, :]
    lag = np.clip(t_idx - s_idx, 0, t_len - 1)
    toe = jnp.where((t_idx >= s_idx)[:, :, None, None, None], taps[lag], 0.0)
    eye = jnp.eye(per, dtype=F32)
    toe = toe.reshape(t_len, t_len, nv, per, SSM_GROUP, SSM_GROUP).transpose(2, 0, 3, 5, 1, 4)
    m_op = (toe[:, :, :, :, :, None, :] * eye[None, None, :, None, None, :, None]).astype(BF16)
    m_op = m_op.reshape(nv, t_len * LANES, t_len * LANES)
    lb = jnp.stack([lbr[:t_len][::-1], lbi[:t_len][::-1]], axis=0)
    lb = lb.reshape(2, t_len, nv, per, n_state, SSM_GROUP).transpose(2, 1, 3, 5, 0, 4)
    w_in_op = (lb[:, :, :, :, None, :, :] * eye[None, None, :, None, :, None, None]).astype(BF16)
    w_in_op = w_in_op.reshape(nv, t_len * LANES, per * 2 * n_state)
    p_re = pw_re[1:, :, None, :]
    p_im = pw_im[1:, :, None, :]
    clr = c_re[None] * p_re - c_im[None] * p_im
    cli = c_re[None] * p_im + c_im[None] * p_re
    cl = jnp.stack([clr, -cli], axis=0)
    cl = cl.reshape(2, t_len, nv, per, SSM_GROUP, n_state).transpose(2, 3, 0, 5, 1, 4)
    w_out_op = (cl[:, :, :, :, :, None, :] * eye[None, :, None, None, None, :, None]).astype(BF16)
    w_out_op = w_out_op.reshape(nv, per * 2 * n_state, t_len * LANES)
    pa_re, pa_im = lam_pow(t_len * (2 ** jnp.arange(n_steps)))
    by_block = lambda a: a.reshape(n_steps, nv, per * 2 * n_state).transpose(1, 0, 2)
    a1 = by_block(jnp.concatenate([pa_re, pa_re], axis=-1))
    a2 = by_block(jnp.concatenate([-pa_im, pa_im], axis=-1))
    d_t = d_skip.reshape(nv, 1, LANES)
    return m_op, w_in_op, w_out_op, a1, a2, d_t


def _combine_body(o_ref, gy_ref, sg_ref, x_ref, wap_ref, wglu_ref, bglu_ref, wsp_ref, wout_ref, g2_ref,
                  h_ref, hn_ref):
    attn = _dot(o_ref[0], wap_ref[...])
    g = gy_ref[0]
    s = g * jax.nn.sigmoid(_dot(g.astype(BF16), wglu_ref[...]) + bglu_ref[...])
    ssm = _dot(s.astype(BF16), wsp_ref[...])
    sg = sg_ref[0]
    d_model = attn.shape[1]
    mix = sg[:, :d_model] * attn + sg[:, d_model:] * ssm
    h = x_ref[0] + _dot(mix.astype(BF16), wout_ref[...])
    h_ref[0] = h
    hn_ref[0] = (h * _rms_scale(h) * g2_ref[...]).astype(BF16)


def _combine(o, gy, sg, x, wap, wglu, bglu, wsp, wout, g2):
    bsz, seq, d_model = x.shape
    tm = TOKEN_TILE
    d_u = gy.shape[-1]
    tile = lambda width: pl.BlockSpec((1, tm, width), lambda b, i: (b, i, 0))
    return pl.pallas_call(
        _combine_body,
        name="combine",
        grid=(bsz, seq // tm),
        in_specs=[tile(o.shape[-1]), tile(d_u), tile(2 * d_model), tile(d_model),
                  _resident(wap.shape), _resident(wglu.shape), _resident(bglu.shape),
                  _resident(wsp.shape), _resident(wout.shape), _resident(g2.shape)],
        out_specs=[tile(d_model), tile(d_model)],
        out_shape=[jax.ShapeDtypeStruct(x.shape, F32), jax.ShapeDtypeStruct(x.shape, BF16)],
        compiler_params=pltpu.CompilerParams(
            dimension_semantics=("arbitrary", "arbitrary"), vmem_limit_bytes=VMEM_LIMIT),
    )(o, gy, sg, x, wap, wglu, bglu, wsp, wout, g2)


def _ffn_body(hn_ref, h_ref, wup_ref, cw_ref, cb_ref, wdn_ref, y_ref, carry_ref, act_ref, *, d_ff):
    @pl.when(pl.program_id(1) == 0)
    def _():
        carry_ref[...] = jnp.zeros_like(carry_ref)

    hn = hn_ref[0]
    tm = hn.shape[0]

    def conv_cols(col0):
        cols = pl.ds(col0, FFN_CHUNK)
        up = _dot(hn, wup_ref[:, cols])
        cat = jnp.concatenate([carry_ref[:, cols], up], axis=0)
        carry_ref[:, cols] = up[tm - SUBLANES:, :]
        out = cb_ref[:, cols]
        for tap in range(CONV_WIDTH):
            lo = SUBLANES - (CONV_WIDTH - 1) + tap
            out = out + cat[lo:lo + tm, :] * cw_ref[tap:tap + 1, cols]
        return out

    for j in range(d_ff // FFN_CHUNK):
        gate = conv_cols(j * FFN_CHUNK)
        val = conv_cols(d_ff + j * FFN_CHUNK)
        act_ref[:, j * FFN_CHUNK:(j + 1) * FFN_CHUNK] = (gate * jax.nn.sigmoid(gate) * val).astype(BF16)
    y_ref[0] = h_ref[0] + _dot(act_ref[...], wdn_ref[...])


def _convffn(hn, h, wup, cw, cb, wdn):
    bsz, seq, d_model = h.shape
    tm = TOKEN_TILE
    d_ff = wdn.shape[0]
    assert d_ff % FFN_CHUNK == 0 and wup.shape[1] == 2 * d_ff
    tile = pl.BlockSpec((1, tm, d_model), lambda b, i: (b, i, 0))
    body = functools.partial(_ffn_body, d_ff=d_ff)
    return pl.pallas_call(
        body,
        name="convffn",
        grid=(bsz, seq // tm),
        in_specs=[tile, tile, _resident(wup.shape), _resident(cw.shape), _resident(cb.shape),
                  _resident(wdn.shape)],
        out_specs=tile,
        out_shape=jax.ShapeDtypeStruct(h.shape, F32),
        scratch_shapes=[pltpu.VMEM((SUBLANES, 2 * d_ff), F32), pltpu.VMEM((tm, d_ff), BF16)],
        compiler_params=pltpu.CompilerParams(
            dimension_semantics=("arbitrary", "arbitrary"), vmem_limit_bytes=VMEM_LIMIT),
    )(hn, h, wup, cw, cb, wdn)


def kernel(x, norm1_gain, w_in, q_norm_gain, k_norm_gain, lambda_q1, lambda_k1, lambda_q2, lambda_k2,
           subln_gain, w_attn_proj, ssm_a_re, ssm_a_im, ssm_log_dt, ssm_b_re, ssm_b_im, ssm_c_re, ssm_c_im,
           ssm_d, w_glu, b_glu, w_ssm_proj, w_out, norm2_gain, w_up, conv_w, conv_b, w_down):
    bsz, seq, d_model = x.shape
    assert norm1_gain.shape[0] == 1, "single-layer trunk"
    assert seq % (TOKEN_TILE) == 0 and seq % SSM_CHUNK == 0
    d_qk = N_HEADS * HEAD_COLS

    perm, d_of_lane = _qk_column_permutation()
    cols = np.arange(w_in.shape[-1])
    cols[:d_qk] = perm
    cols[d_qk:2 * d_qk] = d_qk + perm
    w_in_b = w_in[0][:, cols].astype(BF16)
    lane_gain = lambda g: jnp.tile(g[d_of_lane], N_HEADS)[None, :]
    gq = lane_gain(q_norm_gain[0].astype(F32)) * (LOG2E / math.sqrt(HEAD_DIM))
    gk = lane_gain(k_norm_gain[0].astype(F32))

    pos = jnp.arange(seq, dtype=F32)
    inv_freq = 1.0 / (ROPE_THETA ** (jnp.arange(0, HEAD_DIM, 2, dtype=F32) / HEAD_DIM))
    ang = pos[:, None] * inv_freq[None, :]
    cos_t = jnp.tile(jnp.cos(ang), (1, HEAD_COLS // ROPE_HALF))
    sin_h = jnp.sin(ang)
    sin_t = jnp.concatenate([-sin_h, -sin_h, sin_h, sin_h], axis=1)

    comp = np.tile(_head_lane_component(), N_HEADS)
    head = np.repeat(np.arange(N_HEADS), HEAD_COLS)
    e_np = np.zeros((d_qk, LANES), np.float32)
    e_np[np.arange(d_qk), head * 2 + comp] = 1.0
    e_mat = jnp.asarray(e_np, BF16)
    et_mat = jnp.asarray(e_np.T, BF16)

    qT, k, vT, u, sg = _inproj(x, norm1_gain.astype(F32), w_in_b, gq, gk, cos_t, sin_t, e_mat, et_mat)

    lam = (jnp.exp(jnp.sum(lambda_q1[0].astype(F32) * lambda_k1[0].astype(F32)))
           - jnp.exp(jnp.sum(lambda_q2[0].astype(F32) * lambda_k2[0].astype(F32))) + LAM_INIT)
    score_bound = HEAD_DIM * jnp.max(jnp.abs(gq)) * jnp.max(jnp.abs(gk))
    o = lax.cond(score_bound <= SCORE_LOG2_LIMIT,
                 functools.partial(_attention, bounded=True),
                 functools.partial(_attention, bounded=False),
                 lam.reshape(1, 1), qT, k, vT, subln_gain[0].astype(F32)[:, None])

    n_steps = int(math.log2(min(SSM_TILE, seq) // SSM_CHUNK))
    ops = _ssm_operators(ssm_a_re[0].astype(F32), ssm_a_im[0].astype(F32), ssm_log_dt[0].astype(F32),
                         ssm_b_re[0].astype(F32), ssm_b_im[0].astype(F32), ssm_c_re[0].astype(F32),
                         ssm_c_im[0].astype(F32), ssm_d[0].astype(F32), n_steps)
    gy = _s5(u, *ops)

    h, hn = _combine(o, gy, sg, x, w_attn_proj[0].astype(BF16), w_glu[0].astype(BF16),
                     b_glu.astype(F32), w_ssm_proj[0].astype(BF16), w_out[0].astype(BF16),
                     norm2_gain.astype(F32))
    return _convffn(hn, h, w_up[0].astype(BF16), conv_w[0].astype(F32), conv_b.astype(F32),
                    w_down[0].astype(BF16))
```

```python
import functools
import math

import numpy as np
import jax
import jax.numpy as jnp
from jax import lax
from jax.experimental import pallas as pl
from jax.experimental.pallas import tpu as pltpu

F32 = jnp.float32
BF16 = jnp.bfloat16

N_HEADS = 8
HEAD_DIM = 64
HEAD_COLS = 2 * HEAD_DIM
ROPE_HALF = HEAD_DIM // 2
V_DIM = 128
ROPE_THETA = 10000.0
SSM_GROUP = 16
SSM_GROUPS = 32
SSM_STATE = 64
SSM_CHUNK = 16
CONV_WIDTH = 3
EPS = 1e-6
LAM_INIT = 0.8 - 0.6 * math.exp(-0.3 * 0)

LANES = 128
SUBLANES = 8
TOKEN_TILE = 512
FFN_CHUNK = 256
ATTN_UNROLL = 4
SSM_TILE = 4096
VMEM_LIMIT = 56 * 1024 * 1024
MASK_VALUE = -1e30
SCORE_LOG2_LIMIT = 40.0
LOG2E = math.log2(math.e)


def _dot(a, b):
    return jnp.dot(a, b, preferred_element_type=F32)


def _rms_scale(x):
    return lax.rsqrt(jnp.mean(x * x, axis=-1, keepdims=True) + EPS)


def _resident(shape):
    return pl.BlockSpec(shape, lambda *_: (0,) * len(shape), pipeline_mode=pl.Buffered(1))


def _head_lane_component():
    j = np.arange(HEAD_COLS)
    return (j // ROPE_HALF) % 2


def _qk_column_permutation():
    j = np.arange(HEAD_COLS)
    comp = _head_lane_component()
    d = (j // HEAD_DIM) * ROPE_HALF + j % ROPE_HALF
    within = comp * HEAD_DIM + d
    return (np.arange(N_HEADS)[:, None] * HEAD_COLS + within[None, :]).reshape(-1), d


def _inproj_body(x_ref, g1_ref, w_ref, gq_ref, gk_ref, cos_ref, sin_ref, e_ref, et_ref,
                 qT_ref, k_ref, vT_ref, u_ref, sg_ref, *, d_qk, d_v, d_u):
    x = x_ref[0]
    xn = (x * _rms_scale(x) * g1_ref[...]).astype(BF16)
    cos = cos_ref[...]
    sin = sin_ref[...]

    def norm_rope(col0, gain_ref):
        t = _dot(xn, w_ref[:, col0:col0 + d_qk])
        ss = _dot((t * t).astype(BF16), e_ref[...])
        r = lax.rsqrt(ss * (1.0 / HEAD_DIM) + EPS)
        r_hi = r.astype(BF16)
        r_lo = (r - r_hi.astype(F32)).astype(BF16)
        rf = _dot(jnp.concatenate([r_hi, r_lo], axis=1), et_ref[...])
        t = t * rf * gain_ref[...]
        heads = []
        for h in range(N_HEADS):
            th = t[:, h * HEAD_COLS:(h + 1) * HEAD_COLS]
            heads.append(th * cos + pltpu.roll(th, HEAD_DIM, axis=1) * sin)
        return jnp.concatenate(heads, axis=1)

    q = norm_rope(0, gq_ref)
    qT_ref[0, 0] = q.T.astype(BF16)
    k = norm_rope(d_qk, gk_ref)
    k_ref[0] = k.astype(BF16)
    c0 = 2 * d_qk
    v = _dot(xn, w_ref[:, c0:c0 + d_v])
    vT_ref[0, 0] = v.T.astype(BF16)
    c0 += d_v
    u_ref[0] = _dot(xn, w_ref[:, c0:c0 + d_u])
    c0 += d_u
    sg_ref[0] = jax.nn.sigmoid(_dot(xn, w_ref[:, c0:]))


def _inproj(x, g1, w_in_b, gq, gk, cos_t, sin_t, e_mat, et_mat):
    bsz, seq, d_model = x.shape
    tm = TOKEN_TILE
    nt = seq // tm
    d_qk = N_HEADS * HEAD_COLS
    d_v = N_HEADS * V_DIM
    d_u = SSM_GROUPS * SSM_GROUP
    d_g = 2 * d_model
    in_cols = w_in_b.shape[1]
    assert in_cols == 2 * d_qk + d_v + d_u + d_g and seq % tm == 0
    body = functools.partial(_inproj_body, d_qk=d_qk, d_v=d_v, d_u=d_u)
    return pl.pallas_call(
        body,
        name="inproj",
        grid=(bsz, nt),
        in_specs=[
            pl.BlockSpec((1, tm, d_model), lambda b, i: (b, i, 0)),
            _resident((1, d_model)),
            _resident((d_model, in_cols)),
            _resident((1, d_qk)),
            _resident((1, d_qk)),
            pl.BlockSpec((tm, LANES), lambda b, i: (i, 0)),
            pl.BlockSpec((tm, LANES), lambda b, i: (i, 0)),
            _resident((d_qk, LANES)),
            _resident((2 * LANES, d_qk)),
        ],
        out_specs=[
            pl.BlockSpec((1, 1, d_qk, tm), lambda b, i: (b, i, 0, 0)),
            pl.BlockSpec((1, tm, d_qk), lambda b, i: (b, i, 0)),
            pl.BlockSpec((1, 1, d_v, tm), lambda b, i: (b, i, 0, 0)),
            pl.BlockSpec((1, tm, d_u), lambda b, i: (b, i, 0)),
            pl.BlockSpec((1, tm, d_g), lambda b, i: (b, i, 0)),
        ],
        out_shape=[
            jax.ShapeDtypeStruct((bsz, nt, d_qk, tm), BF16),
            jax.ShapeDtypeStruct((bsz, seq, d_qk), BF16),
            jax.ShapeDtypeStruct((bsz, nt, d_v, tm), BF16),
            jax.ShapeDtypeStruct((bsz, seq, d_u), F32),
            jax.ShapeDtypeStruct((bsz, seq, d_g), F32),
        ],
        compiler_params=pltpu.CompilerParams(
            dimension_semantics=("arbitrary", "arbitrary"), vmem_limit_bytes=VMEM_LIMIT),
    )(x, g1, w_in_b, gq, gk, cos_t, sin_t, e_mat, et_mat)


def _split_components(qT):
    row = lax.broadcasted_iota(jnp.int32, qT.shape, 0)
    is_c0 = ((row // ROPE_HALF) % 2) == 0
    zero = jnp.zeros_like(qT)
    return jnp.where(is_c0, qT, zero), jnp.where(is_c0, zero, qT)


def _causal(shape):
    return lax.broadcasted_iota(jnp.int32, shape, 0) <= lax.broadcasted_iota(jnp.int32, shape, 1)


def _attn_finish(lam_ref, gsub_ref, o_ref, acc0, acc1, l0, l1):
    lam = lam_ref[0, 0]
    a = acc0 * (1.0 / l0) - lam * (acc1 * (1.0 / l1))
    a = a * lax.rsqrt(jnp.mean(a * a, axis=0, keepdims=True) + EPS)
    a = a * gsub_ref[...] * (1.0 - LAM_INIT)
    o_ref[0] = a.T.astype(BF16)


def _attn_online_body(lam_ref, qT_ref, k_ref, vT_ref, gsub_ref, o_ref, acc0_ref, acc1_ref, *, blk):
    qi = pl.program_id(2)
    q_comp = _split_components(qT_ref[0, 0])
    accs = (acc0_ref, acc1_ref)
    acc0_ref[...] = jnp.zeros_like(acc0_ref)
    acc1_ref[...] = jnp.zeros_like(acc1_ref)

    def block(j, carry, masked):
        kj = k_ref[0, pl.ds(pl.multiple_of(j * blk, blk), blk), :]
        vj = vT_ref[0, j]
        out = []
        for c in range(2):
            m, l = carry[c]
            sT = _dot(kj, q_comp[c])
            if masked:
                sT = jnp.where(_causal(sT.shape), sT, MASK_VALUE)
            m_new = jnp.maximum(m, jnp.max(sT, axis=0, keepdims=True))
            alpha = jnp.exp2(m - m_new)
            p = jnp.exp2(sT - m_new)
            out.append((m_new, alpha * l + jnp.sum(p, axis=0, keepdims=True)))
            accs[c][...] = accs[c][...] * alpha + _dot(vj, p.astype(BF16))
        return tuple(out)

    init = ((jnp.full((1, blk), MASK_VALUE, F32), jnp.zeros((1, blk), F32)),) * 2
    carry = lax.fori_loop(0, qi, lambda j, c: block(j, c, False), init)
    (_, l0), (_, l1) = block(qi, carry, True)
    _attn_finish(lam_ref, gsub_ref, o_ref, acc0_ref[...], acc1_ref[...], l0, l1)


def _attn_bounded_body(lam_ref, qT_ref, k_ref, vT_ref, gsub_ref, o_ref,
                       acc0_ref, acc1_ref, sa_ref, sb_ref, *, blk):
    qi = pl.program_id(2)
    q_comp = _split_components(qT_ref[0, 0])
    accs = (acc0_ref, acc1_ref)
    acc0_ref[...] = jnp.zeros_like(acc0_ref)
    acc1_ref[...] = jnp.zeros_like(acc1_ref)

    def scores(j, s_ref):
        kj = k_ref[0, pl.ds(pl.multiple_of(j * blk, blk), blk), :]
        for c in range(2):
            s_ref[c] = _dot(kj, q_comp[c])

    def consume(j, s_ref, carry, masked):
        vj = vT_ref[0, j]
        out = []
        for c in range(2):
            p = jnp.exp2(s_ref[c])
            if masked:
                p = jnp.where(_causal(p.shape), p, 0.0)
            out.append(carry[c] + jnp.sum(p.reshape(blk // SUBLANES, SUBLANES, blk), axis=0))
            accs[c][...] += _dot(vj, p.astype(BF16))
        return tuple(out)

    bufs = (sa_ref, sb_ref)

    def run(first, count, carry, diagonal_last):
        for n in range(count):
            last = n == count - 1
            if not (last and diagonal_last):
                scores(first + n + 1, bufs[(n + 1) % 2])
            carry = consume(first + n, bufs[n % 2], carry, last and diagonal_last)
        return carry

    scores(0, sa_ref)
    trips = qi // ATTN_UNROLL
    carry = lax.fori_loop(0, trips, lambda t, c: run(ATTN_UNROLL * t, ATTN_UNROLL, c, False),
                          (jnp.zeros((SUBLANES, blk), F32),) * 2)
    first = trips * ATTN_UNROLL
    tails = [functools.partial(run, first, rem + 1, diagonal_last=True) for rem in range(ATTN_UNROLL)]
    carry = lax.switch(qi - first, tails, carry)
    l0, l1 = (jnp.sum(part, axis=0, keepdims=True) for part in carry)
    _attn_finish(lam_ref, gsub_ref, o_ref, acc0_ref[...], acc1_ref[...], l0, l1)


def _attention(lam, qT, k, vT, gsub, *, bounded):
    bsz, nblk, d_qk, blk = qT.shape
    seq = k.shape[1]
    scratch = [pltpu.VMEM((V_DIM, blk), F32), pltpu.VMEM((V_DIM, blk), F32)]
    if bounded:
        body = functools.partial(_attn_bounded_body, blk=blk)
        scratch += [pltpu.VMEM((2, blk, blk), F32), pltpu.VMEM((2, blk, blk), F32)]
    else:
        body = functools.partial(_attn_online_body, blk=blk)
    return pl.pallas_call(
        body,
        name="attention_bounded" if bounded else "attention_online",
        grid=(bsz, N_HEADS, nblk),
        in_specs=[
            pl.BlockSpec(memory_space=pltpu.SMEM),
            pl.BlockSpec((1, 1, HEAD_COLS, blk), lambda b, h, i: (b, i, h, 0)),
            pl.BlockSpec((1, seq, HEAD_COLS), lambda b, h, i: (b, 0, h)),
            pl.BlockSpec((1, nblk, V_DIM, blk), lambda b, h, i: (b, 0, h, 0)),
            pl.BlockSpec((V_DIM, 1), lambda b, h, i: (0, 0)),
        ],
        out_specs=pl.BlockSpec((1, blk, V_DIM), lambda b, h, i: (b, i, h)),
        out_shape=jax.ShapeDtypeStruct((bsz, seq, N_HEADS * V_DIM), BF16),
        scratch_shapes=scratch,
        compiler_params=pltpu.CompilerParams(
            dimension_semantics=("arbitrary", "arbitrary", "arbitrary"), vmem_limit_bytes=VMEM_LIMIT),
    )(lam, qT, k, vT, gsub)


def _gelu_tanh(y):
    return 0.5 * y * (1.0 + jnp.tanh(math.sqrt(2.0 / math.pi) * (y + 0.044715 * (y * y * y))))


def _s5_expand_operators(kc_ref, wc_ref, oc_ref, rep_c_ref, rep_tc_ref, m_ref, win_ref, wout_ref):
    per = LANES // SSM_GROUP
    n = SSM_CHUNK * LANES
    row_group = (lax.broadcasted_iota(jnp.int32, (n, LANES), 0) // SSM_GROUP) % per
    lane_group = lax.broadcasted_iota(jnp.int32, (n, LANES), 1) // SSM_GROUP
    taps = jnp.where(row_group == lane_group, _dot(kc_ref[0], rep_c_ref[...]), 0.0).astype(BF16)
    zero = jnp.zeros((LANES, LANES), BF16)
    for s in range(SSM_CHUNK):
        for t in range(SSM_CHUNK):
            lag = t - s
            m_ref[s * LANES:(s + 1) * LANES, t * LANES:(t + 1) * LANES] = (
                taps[lag * LANES:(lag + 1) * LANES] if lag >= 0 else zero)
    wc = wc_ref[0].astype(F32)
    for j in range(per):
        win_ref[:, j * LANES:(j + 1) * LANES] = jnp.where(row_group == j, wc, 0.0).astype(BF16)
    oc = oc_ref[0]
    width = 2 * LANES
    state_group = lax.broadcasted_iota(jnp.int32, (oc.shape[0], width), 0) // (2 * SSM_STATE)
    out_group = (lax.broadcasted_iota(jnp.int32, (oc.shape[0], width), 1) // SSM_GROUP) % per
    for nb in range(n // width):
        cols = slice(nb * width, (nb + 1) * width)
        wout_ref[:, cols] = jnp.where(state_group == out_group, _dot(oc, rep_tc_ref[:, cols]), 0.0).astype(BF16)


def _s5_body(u_ref, kc_ref, wc_ref, oc_ref, rep_c_ref, rep_tc_ref, a1_ref, a2_ref, d_ref, y_ref,
             carry_ref, m_ref, win_ref, wout_ref, *, n_steps):
    @pl.when((pl.program_id(1) == 0) & (pl.program_id(2) == 0))
    def _():
        _s5_expand_operators(kc_ref, wc_ref, oc_ref, rep_c_ref, rep_tc_ref, m_ref, win_ref, wout_ref)

    @pl.when(pl.program_id(2) == 0)
    def _():
        carry_ref[...] = jnp.zeros_like(carry_ref)

    n_rows = u_ref.shape[1] // SSM_CHUNK
    slabs = [u_ref[0, pl.ds(s, n_rows, stride=SSM_CHUNK), :] for s in range(SSM_CHUNK)]
    ucat = jnp.concatenate([slab.astype(BF16) for slab in slabs], axis=1)
    x = _dot(ucat, win_ref[...])
    rows = lax.broadcasted_iota(jnp.int32, (n_rows, LANES), 0)
    x_prev = []
    for i in range(LANES // SSM_GROUP):
        lanes = slice(i * LANES, (i + 1) * LANES)
        a1 = a1_ref[0, :, lanes]
        a2 = a2_ref[0, :, lanes]
        x_in = carry_ref[:, lanes]

        def times_a(v, k):
            return a1[k:k + 1] * v + a2[k:k + 1] * pltpu.roll(v, SSM_STATE, axis=1)

        xi = x[:, lanes] + jnp.where(rows == 0, times_a(x_in, 0)[0:1], 0.0)
        for k in range(n_steps):
            xi = xi + times_a(jnp.where(rows >= (1 << k), pltpu.roll(xi, 1 << k, axis=0), 0.0), k)
        x_prev.append(jnp.where(rows == 0, x_in[0:1], pltpu.roll(xi, 1, axis=0)))
        carry_ref[:, lanes] = jnp.broadcast_to(xi[n_rows - 1:n_rows], (SUBLANES, LANES))
    x_prev = jnp.concatenate(x_prev, axis=1).astype(BF16)
    y = _dot(ucat, m_ref[...]) + _dot(x_prev, wout_ref[...])
    for t in range(SSM_CHUNK):
        yt = y[:, t * LANES:(t + 1) * LANES] + d_ref[0] * slabs[t]
        y_ref[0, pl.ds(t, n_rows, stride=SSM_CHUNK), :] = _gelu_tanh(yt)


def _s5(u, kc, wc, oc, a1, a2, d_t):
    bsz, seq, width = u.shape
    n_blocks = width // LANES
    tile = min(SSM_TILE, seq)
    n_steps = a1.shape[1]
    assert (1 << n_steps) == tile // SSM_CHUNK and seq % tile == 0
    n_in = SSM_CHUNK * LANES
    n_state = (LANES // SSM_GROUP) * 2 * SSM_STATE
    chan = np.arange(LANES) % SSM_GROUP
    rep_c = jnp.asarray(np.arange(SSM_GROUP)[:, None] == chan[None, :], BF16)
    tc_in = np.arange(SSM_CHUNK * SSM_GROUP)
    tc_out = (np.arange(n_in) // LANES) * SSM_GROUP + np.arange(n_in) % SSM_GROUP
    rep_tc = jnp.asarray(tc_in[:, None] == tc_out[None, :], BF16)
    body = functools.partial(_s5_body, n_steps=n_steps)
    per_block = lambda a: pl.BlockSpec((1,) + a.shape[1:], lambda v, b, i: (v, 0, 0))
    const = lambda a: pl.BlockSpec(a.shape, lambda v, b, i: (0, 0))
    tokens = pl.BlockSpec((1, tile, LANES), lambda v, b, i: (b, i, v))
    return pl.pallas_call(
        body,
        name="s5",
        grid=(n_blocks, bsz, seq // tile),
        in_specs=[tokens, per_block(kc), per_block(wc), per_block(oc), const(rep_c), const(rep_tc),
                  per_block(a1), per_block(a2), per_block(d_t)],
        out_specs=tokens,
        out_shape=jax.ShapeDtypeStruct(u.shape, F32),
        scratch_shapes=[pltpu.VMEM((SUBLANES, n_state), F32), pltpu.VMEM((n_in, n_in), BF16),
                        pltpu.VMEM((n_in, n_state), BF16), pltpu.VMEM((n_state, n_in), BF16)],
        compiler_params=pltpu.CompilerParams(
            dimension_semantics=("arbitrary", "arbitrary", "arbitrary"), vmem_limit_bytes=VMEM_LIMIT),
    )(u, kc, wc, oc, rep_c, rep_tc, a1, a2, d_t)


def _ssm_operators(a_re, a_im, log_dt, b_re, b_im, c_re, c_im, d_skip, n_steps):
    t_len = SSM_CHUNK
    groups, n_state = a_re.shape
    per = LANES // SSM_GROUP
    nv = groups // per
    hi = lax.Precision.HIGHEST
    dt = jnp.exp(log_dt)[:, None]
    mag = jnp.exp(a_re * dt)
    lb_re = mag * jnp.cos(a_im * dt)
    lb_im = mag * jnp.sin(a_im * dt)
    nr, ni = lb_re - 1.0, lb_im
    den = a_re * a_re + a_im * a_im
    cr = (nr * a_re + ni * a_im) / den
    ci = (ni * a_re - nr * a_im) / den
    bb_re = cr[..., None] * b_re - ci[..., None] * b_im
    bb_im = cr[..., None] * b_im + ci[..., None] * b_re

    def lam_pow(n):
        n = n.astype(F32)[:, None, None]
        m = jnp.exp(a_re * dt * n)
        return m * jnp.cos(a_im * dt * n), m * jnp.sin(a_im * dt * n)

    pw_re, pw_im = lam_pow(jnp.arange(t_len + 1))
    lbr = pw_re[..., None] * bb_re[None] - pw_im[..., None] * bb_im[None]
    lbi = pw_re[..., None] * bb_im[None] + pw_im[..., None] * bb_re[None]
    taps = (jnp.einsum('gcp,tgpd->tgcd', c_re, lbr[:t_len], precision=hi)
            - jnp.einsum('gcp,tgpd->tgcd', c_im, lbi[:t_len], precision=hi))
    kc = taps.reshape(t_len, nv, per, SSM_GROUP, SSM_GROUP).transpose(1, 0, 2, 4, 3)
    kc = kc.reshape(nv, t_len * LANES, SSM_GROUP).astype(BF16)
    lb = jnp.stack([lbr[:t_len][::-1], lbi[:t_len][::-1]], axis=0)
    lb = lb.reshape(2, t_len, nv, per, n_state, SSM_GROUP).transpose(2, 1, 3, 5, 0, 4)
    wc = lb.reshape(nv, t_len * LANES, 2 * n_state).astype(BF16)
    p_re = pw_re[1:, :, None, :]
    p_im = pw_im[1:, :, None, :]
    clr = c_re[None] * p_re - c_im[None] * p_im
    cli = c_re[None] * p_im + c_im[None] * p_re
    cl = jnp.stack([clr, -cli], axis=0)
    cl = cl.reshape(2, t_len, nv, per, SSM_GROUP, n_state).transpose(2, 3, 0, 5, 1, 4)
    oc = cl.reshape(nv, per * 2 * n_state, t_len * SSM_GROUP).astype(BF16)
    pa_re, pa_im = lam_pow(t_len * (2 ** jnp.arange(n_steps)))
    by_block = lambda a: a.reshape(n_steps, nv, per * 2 * n_state).transpose(1, 0, 2)
    a1 = by_block(jnp.concatenate([pa_re, pa_re], axis=-1))
    a2 = by_block(jnp.concatenate([-pa_im, pa_im], axis=-1))
    d_t = d_skip.reshape(nv, 1, LANES)
    return kc, wc, oc, a1, a2, d_t


def _combine_body(o_ref, gy_ref, sg_ref, x_ref, wap_ref, wglu_ref, bglu_ref, wsp_ref, wout_ref, g2_ref,
                  h_ref, hn_ref):
    attn = _dot(o_ref[0], wap_ref[...])
    g = gy_ref[0]
    s = g * jax.nn.sigmoid(_dot(g.astype(BF16), wglu_ref[...]) + bglu_ref[...])
    ssm = _dot(s.astype(BF16), wsp_ref[...])
    sg = sg_ref[0]
    d_model = attn.shape[1]
    mix = sg[:, :d_model] * attn + sg[:, d_model:] * ssm
    h = x_ref[0] + _dot(mix.astype(BF16), wout_ref[...])
    h_ref[0] = h
    hn_ref[0] = (h * _rms_scale(h) * g2_ref[...]).astype(BF16)


def _combine(o, gy, sg, x, wap, wglu, bglu, wsp, wout, g2):
    bsz, seq, d_model = x.shape
    tm = TOKEN_TILE
    d_u = gy.shape[-1]
    tile = lambda width: pl.BlockSpec((1, tm, width), lambda b, i: (b, i, 0))
    return pl.pallas_call(
        _combine_body,
        name="combine",
        grid=(bsz, seq // tm),
        in_specs=[tile(o.shape[-1]), tile(d_u), tile(2 * d_model), tile(d_model),
                  _resident(wap.shape), _resident(wglu.shape), _resident(bglu.shape),
                  _resident(wsp.shape), _resident(wout.shape), _resident(g2.shape)],
        out_specs=[tile(d_model), tile(d_model)],
        out_shape=[jax.ShapeDtypeStruct(x.shape, F32), jax.ShapeDtypeStruct(x.shape, BF16)],
        compiler_params=pltpu.CompilerParams(
            dimension_semantics=("arbitrary", "arbitrary"), vmem_limit_bytes=VMEM_LIMIT),
    )(o, gy, sg, x, wap, wglu, bglu, wsp, wout, g2)


def _ffn_body(hn_ref, h_ref, wup_ref, cw_ref, cb_ref, wdn_ref, y_ref, carry_ref, act_ref, *, d_ff):
    @pl.when(pl.program_id(1) == 0)
    def _():
        carry_ref[...] = jnp.zeros_like(carry_ref)

    hn = hn_ref[0]
    tm = hn.shape[0]

    def conv_cols(col0):
        cols = pl.ds(col0, FFN_CHUNK)
        up = _dot(hn, wup_ref[:, cols])
        cat = jnp.concatenate([carry_ref[:, cols], up], axis=0)
        carry_ref[:, cols] = up[tm - SUBLANES:, :]
        out = cb_ref[:, cols]
        for tap in range(CONV_WIDTH):
            lo = SUBLANES - (CONV_WIDTH - 1) + tap
            out = out + cat[lo:lo + tm, :] * cw_ref[tap:tap + 1, cols]
        return out

    for j in range(d_ff // FFN_CHUNK):
        gate = conv_cols(j * FFN_CHUNK)
        val = conv_cols(d_ff + j * FFN_CHUNK)
        act_ref[:, j * FFN_CHUNK:(j + 1) * FFN_CHUNK] = (gate * jax.nn.sigmoid(gate) * val).astype(BF16)
    y_ref[0] = h_ref[0] + _dot(act_ref[...], wdn_ref[...])


def _convffn(hn, h, wup, cw, cb, wdn):
    bsz, seq, d_model = h.shape
    tm = TOKEN_TILE
    d_ff = wdn.shape[0]
    assert d_ff % FFN_CHUNK == 0 and wup.shape[1] == 2 * d_ff
    tile = pl.BlockSpec((1, tm, d_model), lambda b, i: (b, i, 0))
    body = functools.partial(_ffn_body, d_ff=d_ff)
    return pl.pallas_call(
        body,
        name="convffn",
        grid=(bsz, seq // tm),
        in_specs=[tile, tile, _resident(wup.shape), _resident(cw.shape), _resident(cb.shape),
                  _resident(wdn.shape)],
        out_specs=tile,
        out_shape=jax.ShapeDtypeStruct(h.shape, F32),
        scratch_shapes=[pltpu.VMEM((SUBLANES, 2 * d_ff), F32), pltpu.VMEM((tm, d_ff), BF16)],
        compiler_params=pltpu.CompilerParams(
            dimension_semantics=("arbitrary", "arbitrary"), vmem_limit_bytes=VMEM_LIMIT),
    )(hn, h, wup, cw, cb, wdn)


def kernel(x, norm1_gain, w_in, q_norm_gain, k_norm_gain, lambda_q1, lambda_k1, lambda_q2, lambda_k2,
           subln_gain, w_attn_proj, ssm_a_re, ssm_a_im, ssm_log_dt, ssm_b_re, ssm_b_im, ssm_c_re, ssm_c_im,
           ssm_d, w_glu, b_glu, w_ssm_proj, w_out, norm2_gain, w_up, conv_w, conv_b, w_down):
    bsz, seq, d_model = x.shape
    assert norm1_gain.shape[0] == 1, "single-layer trunk"
    assert seq % (TOKEN_TILE) == 0 and seq % SSM_CHUNK == 0
    d_qk = N_HEADS * HEAD_COLS

    perm, d_of_lane = _qk_column_permutation()
    cols = np.arange(w_in.shape[-1])
    cols[:d_qk] = perm
    cols[d_qk:2 * d_qk] = d_qk + perm
    w_in_b = w_in[0][:, cols].astype(BF16)
    lane_gain = lambda g: jnp.tile(g[d_of_lane], N_HEADS)[None, :]
    gq = lane_gain(q_norm_gain[0].astype(F32)) * (LOG2E / math.sqrt(HEAD_DIM))
    gk = lane_gain(k_norm_gain[0].astype(F32))

    pos = jnp.arange(seq, dtype=F32)
    inv_freq = 1.0 / (ROPE_THETA ** (jnp.arange(0, HEAD_DIM, 2, dtype=F32) / HEAD_DIM))
    ang = pos[:, None] * inv_freq[None, :]
    cos_t = jnp.tile(jnp.cos(ang), (1, HEAD_COLS // ROPE_HALF))
    sin_h = jnp.sin(ang)
    sin_t = jnp.concatenate([-sin_h, -sin_h, sin_h, sin_h], axis=1)

    comp = np.tile(_head_lane_component(), N_HEADS)
    head = np.repeat(np.arange(N_HEADS), HEAD_COLS)
    e_np = np.zeros((d_qk, LANES), np.float32)
    e_np[np.arange(d_qk), head * 2 + comp] = 1.0
    e_mat = jnp.asarray(e_np, BF16)
    et_mat = jnp.asarray(np.concatenate([e_np.T, e_np.T], axis=0), BF16)

    qT, k, vT, u, sg = _inproj(x, norm1_gain.astype(F32), w_in_b, gq, gk, cos_t, sin_t, e_mat, et_mat)

    lam = (jnp.exp(jnp.sum(lambda_q1[0].astype(F32) * lambda_k1[0].astype(F32)))
           - jnp.exp(jnp.sum(lambda_q2[0].astype(F32) * lambda_k2[0].astype(F32))) + LAM_INIT)
    score_bound = HEAD_DIM * jnp.max(jnp.abs(gq)) * jnp.max(jnp.abs(gk))
    o = lax.cond(score_bound <= SCORE_LOG2_LIMIT,
                 functools.partial(_attention, bounded=True),
                 functools.partial(_attention, bounded=False),
                 lam.reshape(1, 1), qT, k, vT, subln_gain[0].astype(F32)[:, None])

    n_steps = int(math.log2(min(SSM_TILE, seq) // SSM_CHUNK))
    ops = _ssm_operators(ssm_a_re[0].astype(F32), ssm_a_im[0].astype(F32), ssm_log_dt[0].astype(F32),
                         ssm_b_re[0].astype(F32), ssm_b_im[0].astype(F32), ssm_c_re[0].astype(F32),
                         ssm_c_im[0].astype(F32), ssm_d[0].astype(F32), n_steps)
    gy = _s5(u, *ops)

    h, hn = _combine(o, gy, sg, x, w_attn_proj[0].astype(BF16), w_glu[0].astype(BF16),
                     b_glu.astype(F32), w_ssm_proj[0].astype(BF16), w_out[0].astype(BF16),
                     norm2_gain.astype(F32))
    return _convffn(hn, h, w_up[0].astype(BF16), conv_w[0].astype(F32), conv_b.astype(F32),
                    w_down[0].astype(BF16))
```

```python
import functools
import math

import numpy as np
import jax
import jax.numpy as jnp
from jax import lax
from jax.experimental import pallas as pl
from jax.experimental.pallas import tpu as pltpu

F32 = jnp.float32
BF16 = jnp.bfloat16

N_HEADS = 8
HEAD_DIM = 64
HEAD_COLS = 2 * HEAD_DIM
ROPE_HALF = HEAD_DIM // 2
V_DIM = 128
ROPE_THETA = 10000.0
SSM_GROUP = 16
SSM_GROUPS = 32
SSM_STATE = 64
SSM_CHUNK = 16
CONV_WIDTH = 3
EPS = 1e-6
LAM_INIT = 0.8 - 0.6 * math.exp(-0.3 * 0)

LANES = 128
SUBLANES = 8
TOKEN_TILE = 512
FFN_CHUNK = 256
FFN_TILE = 512
ATTN_UNROLL = 4
SSM_TILE = 4096
VMEM_LIMIT = 56 * 1024 * 1024
MASK_VALUE = -1e30
SCORE_LOG2_LIMIT = 40.0
LOG2E = math.log2(math.e)


def _dot(a, b):
    return jnp.dot(a, b, preferred_element_type=F32)


def _rms_scale(x):
    return lax.rsqrt(jnp.mean(x * x, axis=-1, keepdims=True) + EPS)


def _resident(shape):
    return pl.BlockSpec(shape, lambda *_: (0,) * len(shape), pipeline_mode=pl.Buffered(1))


def _head_lane_component():
    j = np.arange(HEAD_COLS)
    return (j // ROPE_HALF) % 2


def _qk_column_permutation():
    j = np.arange(HEAD_COLS)
    comp = _head_lane_component()
    d = (j // HEAD_DIM) * ROPE_HALF + j % ROPE_HALF
    within = comp * HEAD_DIM + d
    return (np.arange(N_HEADS)[:, None] * HEAD_COLS + within[None, :]).reshape(-1), d


def _inproj_body(x_ref, g1_ref, w_ref, gq_ref, gk_ref, cos_ref, sin_ref, e_ref, et_ref,
                 qT_ref, k_ref, vT_ref, u_ref, sg_ref, *, d_qk, d_v, d_u):
    x = x_ref[0]
    xn = (x * _rms_scale(x) * g1_ref[...]).astype(BF16)
    cos = cos_ref[...]
    sin = sin_ref[...]

    def norm_rope(col0, gain_ref):
        t = _dot(xn, w_ref[:, col0:col0 + d_qk])
        ss = _dot((t * t).astype(BF16), e_ref[...])
        r = lax.rsqrt(ss * (1.0 / HEAD_DIM) + EPS)
        r_hi = r.astype(BF16)
        r_lo = (r - r_hi.astype(F32)).astype(BF16)
        rf = _dot(jnp.concatenate([r_hi, r_lo], axis=1), et_ref[...])
        t = t * rf * gain_ref[...]
        heads = []
        for h in range(N_HEADS):
            th = t[:, h * HEAD_COLS:(h + 1) * HEAD_COLS]
            heads.append(th * cos + pltpu.roll(th, HEAD_DIM, axis=1) * sin)
        return jnp.concatenate(heads, axis=1)

    q = norm_rope(0, gq_ref)
    qT_ref[0, 0] = q.T.astype(BF16)
    k = norm_rope(d_qk, gk_ref)
    k_ref[0] = k.astype(BF16)
    c0 = 2 * d_qk
    v = _dot(xn, w_ref[:, c0:c0 + d_v])
    vT_ref[0, 0] = v.T.astype(BF16)
    c0 += d_v
    u_ref[0] = _dot(xn, w_ref[:, c0:c0 + d_u])
    c0 += d_u
    sg_ref[0] = jax.nn.sigmoid(_dot(xn, w_ref[:, c0:]))


def _inproj(x, g1, w_in_b, gq, gk, cos_t, sin_t, e_mat, et_mat):
    bsz, seq, d_model = x.shape
    tm = TOKEN_TILE
    nt = seq // tm
    d_qk = N_HEADS * HEAD_COLS
    d_v = N_HEADS * V_DIM
    d_u = SSM_GROUPS * SSM_GROUP
    d_g = 2 * d_model
    in_cols = w_in_b.shape[1]
    assert in_cols == 2 * d_qk + d_v + d_u + d_g and seq % tm == 0
    body = functools.partial(_inproj_body, d_qk=d_qk, d_v=d_v, d_u=d_u)
    return pl.pallas_call(
        body,
        name="inproj",
        grid=(bsz, nt),
        in_specs=[
            pl.BlockSpec((1, tm, d_model), lambda b, i: (b, i, 0)),
            _resident((1, d_model)),
            _resident((d_model, in_cols)),
            _resident((1, d_qk)),
            _resident((1, d_qk)),
            pl.BlockSpec((tm, LANES), lambda b, i: (i, 0)),
            pl.BlockSpec((tm, LANES), lambda b, i: (i, 0)),
            _resident((d_qk, LANES)),
            _resident((2 * LANES, d_qk)),
        ],
        out_specs=[
            pl.BlockSpec((1, 1, d_qk, tm), lambda b, i: (b, i, 0, 0)),
            pl.BlockSpec((1, tm, d_qk), lambda b, i: (b, i, 0)),
            pl.BlockSpec((1, 1, d_v, tm), lambda b, i: (b, i, 0, 0)),
            pl.BlockSpec((1, tm, d_u), lambda b, i: (b, i, 0)),
            pl.BlockSpec((1, tm, d_g), lambda b, i: (b, i, 0)),
        ],
        out_shape=[
            jax.ShapeDtypeStruct((bsz, nt, d_qk, tm), BF16),
            jax.ShapeDtypeStruct((bsz, seq, d_qk), BF16),
            jax.ShapeDtypeStruct((bsz, nt, d_v, tm), BF16),
            jax.ShapeDtypeStruct((bsz, seq, d_u), F32),
            jax.ShapeDtypeStruct((bsz, seq, d_g), F32),
        ],
        compiler_params=pltpu.CompilerParams(
            dimension_semantics=("arbitrary", "arbitrary"), vmem_limit_bytes=VMEM_LIMIT),
    )(x, g1, w_in_b, gq, gk, cos_t, sin_t, e_mat, et_mat)


def _split_components(qT):
    row = lax.broadcasted_iota(jnp.int32, qT.shape, 0)
    is_c0 = ((row // ROPE_HALF) % 2) == 0
    zero = jnp.zeros_like(qT)
    return jnp.where(is_c0, qT, zero), jnp.where(is_c0, zero, qT)


def _causal(shape):
    return lax.broadcasted_iota(jnp.int32, shape, 0) <= lax.broadcasted_iota(jnp.int32, shape, 1)


def _attn_finish(lam_ref, gsub_ref, acc0, acc1, l0, l1):
    lam = lam_ref[0, 0]
    a = acc0 * (1.0 / l0) - lam * (acc1 * (1.0 / l1))
    a = a * lax.rsqrt(jnp.mean(a * a, axis=0, keepdims=True) + EPS)
    a = a * gsub_ref[...] * (1.0 - LAM_INIT)
    return a.T.astype(BF16)


def _attn_online_body(lam_ref, qT_ref, k_ref, vT_ref, gsub_ref, o_ref, acc0_ref, acc1_ref, *, blk):
    qi = pl.program_id(2)
    q_comp = _split_components(qT_ref[0, 0])
    accs = (acc0_ref, acc1_ref)
    acc0_ref[...] = jnp.zeros_like(acc0_ref)
    acc1_ref[...] = jnp.zeros_like(acc1_ref)

    def block(j, carry, masked):
        kj = k_ref[0, pl.ds(pl.multiple_of(j * blk, blk), blk), :]
        vj = vT_ref[0, j]
        out = []
        for c in range(2):
            m, l = carry[c]
            sT = _dot(kj, q_comp[c])
            if masked:
                sT = jnp.where(_causal(sT.shape), sT, MASK_VALUE)
            m_new = jnp.maximum(m, jnp.max(sT, axis=0, keepdims=True))
            alpha = jnp.exp2(m - m_new)
            p = jnp.exp2(sT - m_new)
            out.append((m_new, alpha * l + jnp.sum(p, axis=0, keepdims=True)))
            accs[c][...] = accs[c][...] * alpha + _dot(vj, p.astype(BF16))
        return tuple(out)

    init = ((jnp.full((1, blk), MASK_VALUE, F32), jnp.zeros((1, blk), F32)),) * 2
    carry = lax.fori_loop(0, qi, lambda j, c: block(j, c, False), init)
    (_, l0), (_, l1) = block(qi, carry, True)
    o_ref[0] = _attn_finish(lam_ref, gsub_ref, acc0_ref[...], acc1_ref[...], l0, l1)


def _attn_bounded_body(lam_ref, qT_ref, k_ref, vT_ref, gsub_ref, o_ref, acc_ref, sa_ref, sb_ref, *, blk):
    qi = pl.program_id(2)
    heads = tuple(slice(s * HEAD_COLS, (s + 1) * HEAD_COLS) for s in range(2))
    q_comp = [_split_components(qT_ref[0, 0, hs, :]) for hs in heads]
    acc_ref[...] = jnp.zeros_like(acc_ref)
    bufs = (sa_ref, sb_ref)

    def scores(s, j):
        kj = k_ref[0, pl.ds(pl.multiple_of(j * blk, blk), blk), heads[s]]
        for c in range(2):
            bufs[s][c] = _dot(kj, q_comp[s][c])

    def consume(s, j, carry, masked):
        vj = vT_ref[0, j, heads[s], :]
        parts = list(carry)
        for c in range(2):
            p = jnp.exp2(bufs[s][c])
            if masked:
                p = jnp.where(_causal(p.shape), p, 0.0)
            parts[2 * s + c] = parts[2 * s + c] + jnp.sum(p.reshape(blk // SUBLANES, SUBLANES, blk), axis=0)
            acc_ref[s, c] += _dot(vj, p.astype(BF16))
        return tuple(parts)

    def run(first, count, carry, diagonal_last):
        n_units = 2 * count
        for u in range(n_units):
            if u + 1 < n_units or not diagonal_last:
                scores((u + 1) % 2, first + (u + 1) // 2)
            carry = consume(u % 2, first + u // 2, carry, diagonal_last and u // 2 == count - 1)
        return carry

    scores(0, 0)
    trips = qi // ATTN_UNROLL
    carry = lax.fori_loop(0, trips, lambda t, c: run(ATTN_UNROLL * t, ATTN_UNROLL, c, False),
                          (jnp.zeros((SUBLANES, blk), F32),) * 4)
    first = trips * ATTN_UNROLL
    tails = [functools.partial(run, first, rem + 1, diagonal_last=True) for rem in range(ATTN_UNROLL)]
    carry = lax.switch(qi - first, tails, carry)
    l = [jnp.sum(part, axis=0, keepdims=True) for part in carry]
    for s in range(2):
        o_ref[0, :, heads[s]] = _attn_finish(lam_ref, gsub_ref, acc_ref[s, 0], acc_ref[s, 1], l[2 * s], l[2 * s + 1])


def _attention(lam, qT, k, vT, gsub, *, bounded):
    bsz, nblk, d_qk, blk = qT.shape
    seq = k.shape[1]
    if bounded:
        group = 2
        body = functools.partial(_attn_bounded_body, blk=blk)
        scratch = [pltpu.VMEM((group, 2, V_DIM, blk), F32),
                   pltpu.VMEM((2, blk, blk), F32), pltpu.VMEM((2, blk, blk), F32)]
    else:
        group = 1
        body = functools.partial(_attn_online_body, blk=blk)
        scratch = [pltpu.VMEM((V_DIM, blk), F32), pltpu.VMEM((V_DIM, blk), F32)]
    return pl.pallas_call(
        body,
        name="attention_bounded" if bounded else "attention_online",
        grid=(bsz, N_HEADS // group, nblk),
        in_specs=[
            pl.BlockSpec(memory_space=pltpu.SMEM),
            pl.BlockSpec((1, 1, group * HEAD_COLS, blk), lambda b, h, i: (b, i, h, 0)),
            pl.BlockSpec((1, seq, group * HEAD_COLS), lambda b, h, i: (b, 0, h)),
            pl.BlockSpec((1, nblk, group * V_DIM, blk), lambda b, h, i: (b, 0, h, 0)),
            pl.BlockSpec((V_DIM, 1), lambda b, h, i: (0, 0)),
        ],
        out_specs=pl.BlockSpec((1, blk, group * V_DIM), lambda b, h, i: (b, i, h)),
        out_shape=jax.ShapeDtypeStruct((bsz, seq, N_HEADS * V_DIM), BF16),
        scratch_shapes=scratch,
        compiler_params=pltpu.CompilerParams(
            dimension_semantics=("arbitrary", "arbitrary", "arbitrary"), vmem_limit_bytes=VMEM_LIMIT),
    )(lam, qT, k, vT, gsub)


def _gelu_tanh(y):
    return 0.5 * y * (1.0 + jnp.tanh(math.sqrt(2.0 / math.pi) * (y + 0.044715 * (y * y * y))))


def _s5_expand_operators(kc_ref, wc_ref, oc_ref, rep_c_ref, rep_tc_ref, m_ref, win_ref, wout_ref):
    per = LANES // SSM_GROUP
    n = SSM_CHUNK * LANES
    row_group = (lax.broadcasted_iota(jnp.int32, (n, LANES), 0) // SSM_GROUP) % per
    lane_group = lax.broadcasted_iota(jnp.int32, (n, LANES), 1) // SSM_GROUP
    taps = jnp.where(row_group == lane_group, _dot(kc_ref[0], rep_c_ref[...]), 0.0).astype(BF16)
    zero = jnp.zeros((LANES, LANES), BF16)
    for s in range(SSM_CHUNK):
        for t in range(SSM_CHUNK):
            lag = t - s
            m_ref[s * LANES:(s + 1) * LANES, t * LANES:(t + 1) * LANES] = (
                taps[lag * LANES:(lag + 1) * LANES] if lag >= 0 else zero)
    wc = wc_ref[0].astype(F32)
    for j in range(per):
        win_ref[:, j * LANES:(j + 1) * LANES] = jnp.where(row_group == j, wc, 0.0).astype(BF16)
    oc = oc_ref[0]
    width = 2 * LANES
    state_group = lax.broadcasted_iota(jnp.int32, (oc.shape[0], width), 0) // (2 * SSM_STATE)
    out_group = (lax.broadcasted_iota(jnp.int32, (oc.shape[0], width), 1) // SSM_GROUP) % per
    for nb in range(n // width):
        cols = slice(nb * width, (nb + 1) * width)
        wout_ref[:, cols] = jnp.where(state_group == out_group, _dot(oc, rep_tc_ref[:, cols]), 0.0).astype(BF16)


def _s5_body(u_ref, kc_ref, wc_ref, oc_ref, rep_c_ref, rep_tc_ref, a1_ref, a2_ref, d_ref, y_ref,
             carry_ref, m_ref, win_ref, wout_ref, *, n_steps):
    @pl.when((pl.program_id(1) == 0) & (pl.program_id(2) == 0))
    def _():
        _s5_expand_operators(kc_ref, wc_ref, oc_ref, rep_c_ref, rep_tc_ref, m_ref, win_ref, wout_ref)

    @pl.when(pl.program_id(2) == 0)
    def _():
        carry_ref[...] = jnp.zeros_like(carry_ref)

    n_rows = u_ref.shape[1] // SSM_CHUNK
    slabs = [u_ref[0, pl.ds(s, n_rows, stride=SSM_CHUNK), :] for s in range(SSM_CHUNK)]
    ucat = jnp.concatenate([slab.astype(BF16) for slab in slabs], axis=1)
    x = _dot(ucat, win_ref[...])
    rows = lax.broadcasted_iota(jnp.int32, (n_rows, LANES), 0)
    x_prev = []
    for i in range(LANES // SSM_GROUP):
        lanes = slice(i * LANES, (i + 1) * LANES)
        a1 = a1_ref[0, :, lanes]
        a2 = a2_ref[0, :, lanes]
        x_in = carry_ref[:, lanes]

        def times_a(v, k):
            return a1[k:k + 1] * v + a2[k:k + 1] * pltpu.roll(v, SSM_STATE, axis=1)

        xi = x[:, lanes] + jnp.where(rows == 0, times_a(x_in, 0)[0:1], 0.0)
        for k in range(n_steps):
            xi = xi + times_a(jnp.where(rows >= (1 << k), pltpu.roll(xi, 1 << k, axis=0), 0.0), k)
        x_prev.append(jnp.where(rows == 0, x_in[0:1], pltpu.roll(xi, 1, axis=0)))
        carry_ref[:, lanes] = jnp.broadcast_to(xi[n_rows - 1:n_rows], (SUBLANES, LANES))
    x_prev = jnp.concatenate(x_prev, axis=1).astype(BF16)
    y = _dot(ucat, m_ref[...]) + _dot(x_prev, wout_ref[...])
    for t in range(SSM_CHUNK):
        yt = y[:, t * LANES:(t + 1) * LANES] + d_ref[0] * slabs[t]
        y_ref[0, pl.ds(t, n_rows, stride=SSM_CHUNK), :] = _gelu_tanh(yt)


def _s5(u, kc, wc, oc, a1, a2, d_t):
    bsz, seq, width = u.shape
    n_blocks = width // LANES
    tile = min(SSM_TILE, seq)
    n_steps = a1.shape[1]
    assert (1 << n_steps) == tile // SSM_CHUNK and seq % tile == 0
    n_in = SSM_CHUNK * LANES
    n_state = (LANES // SSM_GROUP) * 2 * SSM_STATE
    chan = np.arange(LANES) % SSM_GROUP
    rep_c = jnp.asarray(np.arange(SSM_GROUP)[:, None] == chan[None, :], BF16)
    tc_in = np.arange(SSM_CHUNK * SSM_GROUP)
    tc_out = (np.arange(n_in) // LANES) * SSM_GROUP + np.arange(n_in) % SSM_GROUP
    rep_tc = jnp.asarray(tc_in[:, None] == tc_out[None, :], BF16)
    body = functools.partial(_s5_body, n_steps=n_steps)
    per_block = lambda a: pl.BlockSpec((1,) + a.shape[1:], lambda v, b, i: (v, 0, 0))
    const = lambda a: pl.BlockSpec(a.shape, lambda v, b, i: (0, 0))
    tokens = pl.BlockSpec((1, tile, LANES), lambda v, b, i: (b, i, v))
    return pl.pallas_call(
        body,
        name="s5",
        grid=(n_blocks, bsz, seq // tile),
        in_specs=[tokens, per_block(kc), per_block(wc), per_block(oc), const(rep_c), const(rep_tc),
                  per_block(a1), per_block(a2), per_block(d_t)],
        out_specs=tokens,
        out_shape=jax.ShapeDtypeStruct(u.shape, F32),
        scratch_shapes=[pltpu.VMEM((SUBLANES, n_state), F32), pltpu.VMEM((n_in, n_in), BF16),
                        pltpu.VMEM((n_in, n_state), BF16), pltpu.VMEM((n_state, n_in), BF16)],
        compiler_params=pltpu.CompilerParams(
            dimension_semantics=("arbitrary", "arbitrary", "arbitrary"), vmem_limit_bytes=VMEM_LIMIT),
    )(u, kc, wc, oc, rep_c, rep_tc, a1, a2, d_t)


def _ssm_operators(a_re, a_im, log_dt, b_re, b_im, c_re, c_im, d_skip, n_steps):
    t_len = SSM_CHUNK
    groups, n_state = a_re.shape
    per = LANES // SSM_GROUP
    nv = groups // per
    hi = lax.Precision.HIGHEST
    dt = jnp.exp(log_dt)[:, None]
    mag = jnp.exp(a_re * dt)
    lb_re = mag * jnp.cos(a_im * dt)
    lb_im = mag * jnp.sin(a_im * dt)
    nr, ni = lb_re - 1.0, lb_im
    den = a_re * a_re + a_im * a_im
    cr = (nr * a_re + ni * a_im) / den
    ci = (ni * a_re - nr * a_im) / den
    bb_re = cr[..., None] * b_re - ci[..., None] * b_im
    bb_im = cr[..., None] * b_im + ci[..., None] * b_re

    def lam_pow(n):
        n = n.astype(F32)[:, None, None]
        m = jnp.exp(a_re * dt * n)
        return m * jnp.cos(a_im * dt * n), m * jnp.sin(a_im * dt * n)

    pw_re, pw_im = lam_pow(jnp.arange(t_len + 1))
    lbr = pw_re[..., None] * bb_re[None] - pw_im[..., None] * bb_im[None]
    lbi = pw_re[..., None] * bb_im[None] + pw_im[..., None] * bb_re[None]
    taps = (jnp.einsum('gcp,tgpd->tgcd', c_re, lbr[:t_len], precision=hi)
            - jnp.einsum('gcp,tgpd->tgcd', c_im, lbi[:t_len], precision=hi))
    kc = taps.reshape(t_len, nv, per, SSM_GROUP, SSM_GROUP).transpose(1, 0, 2, 4, 3)
    kc = kc.reshape(nv, t_len * LANES, SSM_GROUP).astype(BF16)
    lb = jnp.stack([lbr[:t_len][::-1], lbi[:t_len][::-1]], axis=0)
    lb = lb.reshape(2, t_len, nv, per, n_state, SSM_GROUP).transpose(2, 1, 3, 5, 0, 4)
    wc = lb.reshape(nv, t_len * LANES, 2 * n_state).astype(BF16)
    p_re = pw_re[1:, :, None, :]
    p_im = pw_im[1:, :, None, :]
    clr = c_re[None] * p_re - c_im[None] * p_im
    cli = c_re[None] * p_im + c_im[None] * p_re
    cl = jnp.stack([clr, -cli], axis=0)
    cl = cl.reshape(2, t_len, nv, per, SSM_GROUP, n_state).transpose(2, 3, 0, 5, 1, 4)
    oc = cl.reshape(nv, per * 2 * n_state, t_len * SSM_GROUP).astype(BF16)
    pa_re, pa_im = lam_pow(t_len * (2 ** jnp.arange(n_steps)))
    by_block = lambda a: a.reshape(n_steps, nv, per * 2 * n_state).transpose(1, 0, 2)
    a1 = by_block(jnp.concatenate([pa_re, pa_re], axis=-1))
    a2 = by_block(jnp.concatenate([-pa_im, pa_im], axis=-1))
    d_t = d_skip.reshape(nv, 1, LANES)
    return kc, wc, oc, a1, a2, d_t


def _combine_body(o_ref, gy_ref, sg_ref, x_ref, wap_ref, wglu_ref, bglu_ref, wsp_ref, wout_ref, g2_ref,
                  h_ref, hn_ref):
    attn = _dot(o_ref[0], wap_ref[...])
    g = gy_ref[0]
    s = g * jax.nn.sigmoid(_dot(g.astype(BF16), wglu_ref[...]) + bglu_ref[...])
    ssm = _dot(s.astype(BF16), wsp_ref[...])
    sg = sg_ref[0]
    d_model = attn.shape[1]
    mix = sg[:, :d_model] * attn + sg[:, d_model:] * ssm
    h = x_ref[0] + _dot(mix.astype(BF16), wout_ref[...])
    h_ref[0] = h
    hn_ref[0] = (h * _rms_scale(h) * g2_ref[...]).astype(BF16)


def _combine(o, gy, sg, x, wap, wglu, bglu, wsp, wout, g2):
    bsz, seq, d_model = x.shape
    tm = TOKEN_TILE
    d_u = gy.shape[-1]
    tile = lambda width: pl.BlockSpec((1, tm, width), lambda b, i: (b, i, 0))
    return pl.pallas_call(
        _combine_body,
        name="combine",
        grid=(bsz, seq // tm),
        in_specs=[tile(o.shape[-1]), tile(d_u), tile(2 * d_model), tile(d_model),
                  _resident(wap.shape), _resident(wglu.shape), _resident(bglu.shape),
                  _resident(wsp.shape), _resident(wout.shape), _resident(g2.shape)],
        out_specs=[tile(d_model), tile(d_model)],
        out_shape=[jax.ShapeDtypeStruct(x.shape, F32), jax.ShapeDtypeStruct(x.shape, BF16)],
        compiler_params=pltpu.CompilerParams(
            dimension_semantics=("arbitrary", "arbitrary"), vmem_limit_bytes=VMEM_LIMIT),
    )(o, gy, sg, x, wap, wglu, bglu, wsp, wout, g2)


def _ffn_body(hn_ref, h_ref, wup_ref, cw_ref, cb_ref, wdn_ref, y_ref, carry_ref, act_ref, upg_ref, upv_ref,
              *, d_ff):
    @pl.when(pl.program_id(1) == 0)
    def _():
        carry_ref[...] = jnp.zeros_like(carry_ref)

    hn = hn_ref[0]
    tm = hn.shape[0]

    def conv_cols(col0, up_ref):
        cols = pl.ds(col0, FFN_CHUNK)
        up = _dot(hn, wup_ref[:, cols])
        up_ref[0:SUBLANES, :] = carry_ref[:, cols]
        up_ref[SUBLANES:, :] = up
        carry_ref[:, cols] = up[tm - SUBLANES:, :]
        out = cb_ref[:, cols] + up * cw_ref[CONV_WIDTH - 1:CONV_WIDTH, cols]
        for tap in range(CONV_WIDTH - 1):
            lo = SUBLANES - (CONV_WIDTH - 1) + tap
            out = out + up_ref[lo:lo + tm, :] * cw_ref[tap:tap + 1, cols]
        return out

    for j in range(d_ff // FFN_CHUNK):
        gate = conv_cols(j * FFN_CHUNK, upg_ref)
        val = conv_cols(d_ff + j * FFN_CHUNK, upv_ref)
        act_ref[:, j * FFN_CHUNK:(j + 1) * FFN_CHUNK] = (gate * jax.nn.sigmoid(gate) * val).astype(BF16)
    y_ref[0] = h_ref[0] + _dot(act_ref[...], wdn_ref[...])


def _convffn(hn, h, wup, cw, cb, wdn):
    bsz, seq, d_model = h.shape
    tm = FFN_TILE
    d_ff = wdn.shape[0]
    assert d_ff % FFN_CHUNK == 0 and wup.shape[1] == 2 * d_ff
    tile = pl.BlockSpec((1, tm, d_model), lambda b, i: (b, i, 0))
    body = functools.partial(_ffn_body, d_ff=d_ff)
    return pl.pallas_call(
        body,
        name="convffn",
        grid=(bsz, seq // tm),
        in_specs=[tile, tile, _resident(wup.shape), _resident(cw.shape), _resident(cb.shape),
                  _resident(wdn.shape)],
        out_specs=tile,
        out_shape=jax.ShapeDtypeStruct(h.shape, F32),
        scratch_shapes=[pltpu.VMEM((SUBLANES, 2 * d_ff), F32), pltpu.VMEM((tm, d_ff), BF16),
                        pltpu.VMEM((SUBLANES + tm, FFN_CHUNK), F32), pltpu.VMEM((SUBLANES + tm, FFN_CHUNK), F32)],
        compiler_params=pltpu.CompilerParams(
            dimension_semantics=("arbitrary", "arbitrary"), vmem_limit_bytes=VMEM_LIMIT),
    )(hn, h, wup, cw, cb, wdn)


def kernel(x, norm1_gain, w_in, q_norm_gain, k_norm_gain, lambda_q1, lambda_k1, lambda_q2, lambda_k2,
           subln_gain, w_attn_proj, ssm_a_re, ssm_a_im, ssm_log_dt, ssm_b_re, ssm_b_im, ssm_c_re, ssm_c_im,
           ssm_d, w_glu, b_glu, w_ssm_proj, w_out, norm2_gain, w_up, conv_w, conv_b, w_down):
    bsz, seq, d_model = x.shape
    assert norm1_gain.shape[0] == 1, "single-layer trunk"
    assert seq % (TOKEN_TILE) == 0 and seq % SSM_CHUNK == 0
    d_qk = N_HEADS * HEAD_COLS

    perm, d_of_lane = _qk_column_permutation()
    cols = np.arange(w_in.shape[-1])
    cols[:d_qk] = perm
    cols[d_qk:2 * d_qk] = d_qk + perm
    w_in_b = w_in[0][:, cols].astype(BF16)
    lane_gain = lambda g: jnp.tile(g[d_of_lane], N_HEADS)[None, :]
    gq = lane_gain(q_norm_gain[0].astype(F32)) * (LOG2E / math.sqrt(HEAD_DIM))
    gk = lane_gain(k_norm_gain[0].astype(F32))

    pos = jnp.arange(seq, dtype=F32)
    inv_freq = 1.0 / (ROPE_THETA ** (jnp.arange(0, HEAD_DIM, 2, dtype=F32) / HEAD_DIM))
    ang = pos[:, None] * inv_freq[None, :]
    cos_t = jnp.tile(jnp.cos(ang), (1, HEAD_COLS // ROPE_HALF))
    sin_h = jnp.sin(ang)
    sin_t = jnp.concatenate([-sin_h, -sin_h, sin_h, sin_h], axis=1)

    comp = np.tile(_head_lane_component(), N_HEADS)
    head = np.repeat(np.arange(N_HEADS), HEAD_COLS)
    e_np = np.zeros((d_qk, LANES), np.float32)
    e_np[np.arange(d_qk), head * 2 + comp] = 1.0
    e_mat = jnp.asarray(e_np, BF16)
    et_mat = jnp.asarray(np.concatenate([e_np.T, e_np.T], axis=0), BF16)

    qT, k, vT, u, sg = _inproj(x, norm1_gain.astype(F32), w_in_b, gq, gk, cos_t, sin_t, e_mat, et_mat)

    lam = (jnp.exp(jnp.sum(lambda_q1[0].astype(F32) * lambda_k1[0].astype(F32)))
           - jnp.exp(jnp.sum(lambda_q2[0].astype(F32) * lambda_k2[0].astype(F32))) + LAM_INIT)
    score_bound = HEAD_DIM * jnp.max(jnp.abs(gq)) * jnp.max(jnp.abs(gk))
    o = lax.cond(score_bound <= SCORE_LOG2_LIMIT,
                 functools.partial(_attention, bounded=True),
                 functools.partial(_attention, bounded=False),
                 lam.reshape(1, 1), qT, k, vT, subln_gain[0].astype(F32)[:, None])

    n_steps = int(math.log2(min(SSM_TILE, seq) // SSM_CHUNK))
    ops = _ssm_operators(ssm_a_re[0].astype(F32), ssm_a_im[0].astype(F32), ssm_log_dt[0].astype(F32),
                         ssm_b_re[0].astype(F32), ssm_b_im[0].astype(F32), ssm_c_re[0].astype(F32),
                         ssm_c_im[0].astype(F32), ssm_d[0].astype(F32), n_steps)
    gy = _s5(u, *ops)

    h, hn = _combine(o, gy, sg, x, w_attn_proj[0].astype(BF16), w_glu[0].astype(BF16),
                     b_glu.astype(F32), w_ssm_proj[0].astype(BF16), w_out[0].astype(BF16),
                     norm2_gain.astype(F32))
    return _convffn(hn, h, w_up[0].astype(BF16), conv_w[0].astype(F32), conv_b.astype(F32),
                    w_down[0].astype(BF16))
```

```python
import functools
import math

import numpy as np
import jax
import jax.numpy as jnp
from jax import lax
from jax.experimental import pallas as pl
from jax.experimental.pallas import tpu as pltpu

F32 = jnp.float32
BF16 = jnp.bfloat16

N_HEADS = 8
HEAD_DIM = 64
HEAD_COLS = 2 * HEAD_DIM
ROPE_HALF = HEAD_DIM // 2
V_DIM = 128
ROPE_THETA = 10000.0
SSM_GROUP = 16
SSM_GROUPS = 32
SSM_STATE = 64
SSM_CHUNK = 16
CONV_WIDTH = 3
EPS = 1e-6
LAM_INIT = 0.8 - 0.6 * math.exp(-0.3 * 0)

LANES = 128
SUBLANES = 8
TOKEN_TILE = 512
FFN_CHUNK = 256
FFN_TILE = 512
CAST_ROWS = 256
ATTN_UNROLL = 4
SSM_TILE = 4096
VMEM_LIMIT = 56 * 1024 * 1024
MASK_VALUE = -1e30
SCORE_LOG2_LIMIT = 40.0
LOG2E = math.log2(math.e)


def _dot(a, b):
    return jnp.dot(a, b, preferred_element_type=F32)


def _rms_scale(x):
    return lax.rsqrt(jnp.mean(x * x, axis=-1, keepdims=True) + EPS)


def _resident(shape):
    return pl.BlockSpec(shape, lambda *_: (0,) * len(shape), pipeline_mode=pl.Buffered(1))


def _to_bf16_body(w_ref, o_ref, *, qk_cols):
    w = w_ref[...]
    if qk_cols:
        head = w[:, :qk_cols]
        quarter = (lax.broadcasted_iota(jnp.int32, head.shape, 1) % HEAD_COLS) // ROPE_HALF
        from_right = pltpu.roll(head, qk_cols - ROPE_HALF, axis=1)
        from_left = pltpu.roll(head, ROPE_HALF, axis=1)
        head = jnp.where(quarter == 1, from_right, jnp.where(quarter == 2, from_left, head))
        o_ref[:, :qk_cols] = head.astype(BF16)
        o_ref[:, qk_cols:] = w[:, qk_cols:].astype(BF16)
    else:
        o_ref[...] = w.astype(BF16)


def _to_bf16(w, qk_cols=0):
    rows, cols = w.shape
    blk = min(rows, CAST_ROWS)
    assert rows % blk == 0
    return pl.pallas_call(
        functools.partial(_to_bf16_body, qk_cols=qk_cols),
        name="to_bf16",
        grid=(rows // blk,),
        in_specs=[pl.BlockSpec((blk, cols), lambda i: (i, 0))],
        out_specs=pl.BlockSpec((blk, cols), lambda i: (i, 0)),
        out_shape=jax.ShapeDtypeStruct(w.shape, BF16),
        compiler_params=pltpu.CompilerParams(dimension_semantics=("arbitrary",), vmem_limit_bytes=VMEM_LIMIT),
    )(w)


def _head_lane_component():
    j = np.arange(HEAD_COLS)
    return (j // ROPE_HALF) % 2


def _head_lane_dim():
    j = np.arange(HEAD_COLS)
    return (j // HEAD_DIM) * ROPE_HALF + j % ROPE_HALF


def _inproj_body(x_ref, g1_ref, w_ref, gq_ref, gk_ref, cos_ref, sin_ref, e_ref, et_ref,
                 qT_ref, k_ref, vT_ref, u_ref, sg_ref, *, d_qk, d_v, d_u):
    x = x_ref[0]
    xn = (x * _rms_scale(x) * g1_ref[...]).astype(BF16)
    cos = cos_ref[...]
    sin = sin_ref[...]

    def norm_rope(col0, gain_ref):
        t = _dot(xn, w_ref[:, col0:col0 + d_qk])
        ss = _dot((t * t).astype(BF16), e_ref[...])
        r = lax.rsqrt(ss * (1.0 / HEAD_DIM) + EPS)
        r_hi = r.astype(BF16)
        r_lo = (r - r_hi.astype(F32)).astype(BF16)
        rf = _dot(jnp.concatenate([r_hi, r_lo], axis=1), et_ref[...])
        t = t * rf * gain_ref[...]
        heads = []
        for h in range(N_HEADS):
            th = t[:, h * HEAD_COLS:(h + 1) * HEAD_COLS]
            heads.append(th * cos + pltpu.roll(th, HEAD_DIM, axis=1) * sin)
        return jnp.concatenate(heads, axis=1)

    q = norm_rope(0, gq_ref)
    qT_ref[0, 0] = q.T.astype(BF16)
    k = norm_rope(d_qk, gk_ref)
    k_ref[0] = k.astype(BF16)
    c0 = 2 * d_qk
    v = _dot(xn, w_ref[:, c0:c0 + d_v])
    vT_ref[0, 0] = v.T.astype(BF16)
    c0 += d_v
    u_ref[0] = _dot(xn, w_ref[:, c0:c0 + d_u])
    c0 += d_u
    sg_ref[0] = jax.nn.sigmoid(_dot(xn, w_ref[:, c0:]))


def _inproj(x, g1, w_in_b, gq, gk, cos_t, sin_t, e_mat, et_mat):
    bsz, seq, d_model = x.shape
    tm = TOKEN_TILE
    nt = seq // tm
    d_qk = N_HEADS * HEAD_COLS
    d_v = N_HEADS * V_DIM
    d_u = SSM_GROUPS * SSM_GROUP
    d_g = 2 * d_model
    in_cols = w_in_b.shape[1]
    assert in_cols == 2 * d_qk + d_v + d_u + d_g and seq % tm == 0
    body = functools.partial(_inproj_body, d_qk=d_qk, d_v=d_v, d_u=d_u)
    return pl.pallas_call(
        body,
        name="inproj",
        grid=(bsz, nt),
        in_specs=[
            pl.BlockSpec((1, tm, d_model), lambda b, i: (b, i, 0)),
            _resident((1, d_model)),
            _resident((d_model, in_cols)),
            _resident((1, d_qk)),
            _resident((1, d_qk)),
            pl.BlockSpec((tm, LANES), lambda b, i: (i, 0)),
            pl.BlockSpec((tm, LANES), lambda b, i: (i, 0)),
            _resident((d_qk, LANES)),
            _resident((2 * LANES, d_qk)),
        ],
        out_specs=[
            pl.BlockSpec((1, 1, d_qk, tm), lambda b, i: (b, i, 0, 0)),
            pl.BlockSpec((1, tm, d_qk), lambda b, i: (b, i, 0)),
            pl.BlockSpec((1, 1, d_v, tm), lambda b, i: (b, i, 0, 0)),
            pl.BlockSpec((1, tm, d_u), lambda b, i: (b, i, 0)),
            pl.BlockSpec((1, tm, d_g), lambda b, i: (b, i, 0)),
        ],
        out_shape=[
            jax.ShapeDtypeStruct((bsz, nt, d_qk, tm), BF16),
            jax.ShapeDtypeStruct((bsz, seq, d_qk), BF16),
            jax.ShapeDtypeStruct((bsz, nt, d_v, tm), BF16),
            jax.ShapeDtypeStruct((bsz, seq, d_u), F32),
            jax.ShapeDtypeStruct((bsz, seq, d_g), F32),
        ],
        compiler_params=pltpu.CompilerParams(
            dimension_semantics=("arbitrary", "arbitrary"), vmem_limit_bytes=VMEM_LIMIT),
    )(x, g1, w_in_b, gq, gk, cos_t, sin_t, e_mat, et_mat)


def _split_components(qT):
    row = lax.broadcasted_iota(jnp.int32, qT.shape, 0)
    is_c0 = ((row // ROPE_HALF) % 2) == 0
    zero = jnp.zeros_like(qT)
    return jnp.where(is_c0, qT, zero), jnp.where(is_c0, zero, qT)


def _causal(shape):
    return lax.broadcasted_iota(jnp.int32, shape, 0) <= lax.broadcasted_iota(jnp.int32, shape, 1)


def _attn_finish(lam_ref, gsub_ref, acc0, acc1, l0, l1):
    lam = lam_ref[0, 0]
    a = acc0 * (1.0 / l0) - lam * (acc1 * (1.0 / l1))
    a = a * lax.rsqrt(jnp.mean(a * a, axis=0, keepdims=True) + EPS)
    a = a * gsub_ref[...] * (1.0 - LAM_INIT)
    return a.T.astype(BF16)


def _attn_online_body(lam_ref, qT_ref, k_ref, vT_ref, gsub_ref, o_ref, acc0_ref, acc1_ref, *, blk):
    qi = pl.program_id(2)
    q_comp = _split_components(qT_ref[0, 0])
    accs = (acc0_ref, acc1_ref)
    acc0_ref[...] = jnp.zeros_like(acc0_ref)
    acc1_ref[...] = jnp.zeros_like(acc1_ref)

    def block(j, carry, masked):
        kj = k_ref[0, pl.ds(pl.multiple_of(j * blk, blk), blk), :]
        vj = vT_ref[0, j]
        out = []
        for c in range(2):
            m, l = carry[c]
            sT = _dot(kj, q_comp[c])
            if masked:
                sT = jnp.where(_causal(sT.shape), sT, MASK_VALUE)
            m_new = jnp.maximum(m, jnp.max(sT, axis=0, keepdims=True))
            alpha = jnp.exp2(m - m_new)
            p = jnp.exp2(sT - m_new)
            out.append((m_new, alpha * l + jnp.sum(p, axis=0, keepdims=True)))
            accs[c][...] = accs[c][...] * alpha + _dot(vj, p.astype(BF16))
        return tuple(out)

    init = ((jnp.full((1, blk), MASK_VALUE, F32), jnp.zeros((1, blk), F32)),) * 2
    carry = lax.fori_loop(0, qi, lambda j, c: block(j, c, False), init)
    (_, l0), (_, l1) = block(qi, carry, True)
    o_ref[0] = _attn_finish(lam_ref, gsub_ref, acc0_ref[...], acc1_ref[...], l0, l1)


def _attn_bounded_body(lam_ref, qT_ref, k_ref, vT_ref, gsub_ref, o_ref, acc_ref, sa_ref, sb_ref, *, blk):
    qi = pl.program_id(2)
    heads = tuple(slice(s * HEAD_COLS, (s + 1) * HEAD_COLS) for s in range(2))
    q_comp = [_split_components(qT_ref[0, 0, hs, :]) for hs in heads]
    acc_ref[...] = jnp.zeros_like(acc_ref)
    bufs = (sa_ref, sb_ref)

    def scores(s, j):
        kj = k_ref[0, pl.ds(pl.multiple_of(j * blk, blk), blk), heads[s]]
        for c in range(2):
            bufs[s][c] = _dot(kj, q_comp[s][c])

    def consume(s, j, carry, masked):
        vj = vT_ref[0, j, heads[s], :]
        parts = list(carry)
        for c in range(2):
            p = jnp.exp2(bufs[s][c])
            if masked:
                p = jnp.where(_causal(p.shape), p, 0.0)
            parts[2 * s + c] = parts[2 * s + c] + jnp.sum(p.reshape(blk // SUBLANES, SUBLANES, blk), axis=0)
            acc_ref[s, c] += _dot(vj, p.astype(BF16))
        return tuple(parts)

    def run(first, count, carry, diagonal_last):
        n_units = 2 * count
        for u in range(n_units):
            if u + 1 < n_units or not diagonal_last:
                scores((u + 1) % 2, first + (u + 1) // 2)
            carry = consume(u % 2, first + u // 2, carry, diagonal_last and u // 2 == count - 1)
        return carry

    scores(0, 0)
    trips = qi // ATTN_UNROLL
    carry = lax.fori_loop(0, trips, lambda t, c: run(ATTN_UNROLL * t, ATTN_UNROLL, c, False),
                          (jnp.zeros((SUBLANES, blk), F32),) * 4)
    first = trips * ATTN_UNROLL
    tails = [functools.partial(run, first, rem + 1, diagonal_last=True) for rem in range(ATTN_UNROLL)]
    carry = lax.switch(qi - first, tails, carry)
    l = [jnp.sum(part, axis=0, keepdims=True) for part in carry]
    for s in range(2):
        o_ref[0, :, heads[s]] = _attn_finish(lam_ref, gsub_ref, acc_ref[s, 0], acc_ref[s, 1], l[2 * s], l[2 * s + 1])


def _attention(lam, qT, k, vT, gsub, *, bounded):
    bsz, nblk, d_qk, blk = qT.shape
    seq = k.shape[1]
    if bounded:
        group = 2
        body = functools.partial(_attn_bounded_body, blk=blk)
        scratch = [pltpu.VMEM((group, 2, V_DIM, blk), F32),
                   pltpu.VMEM((2, blk, blk), F32), pltpu.VMEM((2, blk, blk), F32)]
    else:
        group = 1
        body = functools.partial(_attn_online_body, blk=blk)
        scratch = [pltpu.VMEM((V_DIM, blk), F32), pltpu.VMEM((V_DIM, blk), F32)]
    return pl.pallas_call(
        body,
        name="attention_bounded" if bounded else "attention_online",
        grid=(bsz, N_HEADS // group, nblk),
        in_specs=[
            pl.BlockSpec(memory_space=pltpu.SMEM),
            pl.BlockSpec((1, 1, group * HEAD_COLS, blk), lambda b, h, i: (b, i, h, 0)),
            pl.BlockSpec((1, seq, group * HEAD_COLS), lambda b, h, i: (b, 0, h)),
            pl.BlockSpec((1, nblk, group * V_DIM, blk), lambda b, h, i: (b, 0, h, 0)),
            pl.BlockSpec((V_DIM, 1), lambda b, h, i: (0, 0)),
        ],
        out_specs=pl.BlockSpec((1, blk, group * V_DIM), lambda b, h, i: (b, i, h)),
        out_shape=jax.ShapeDtypeStruct((bsz, seq, N_HEADS * V_DIM), BF16),
        scratch_shapes=scratch,
        compiler_params=pltpu.CompilerParams(
            dimension_semantics=("arbitrary", "arbitrary", "arbitrary"), vmem_limit_bytes=VMEM_LIMIT),
    )(lam, qT, k, vT, gsub)


def _gelu_tanh(y):
    return 0.5 * y * (1.0 + jnp.tanh(math.sqrt(2.0 / math.pi) * (y + 0.044715 * (y * y * y))))


def _s5_expand_operators(kc_ref, wc_ref, oc_ref, rep_c_ref, rep_tc_ref, m_ref, win_ref, wout_ref):
    per = LANES // SSM_GROUP
    n = SSM_CHUNK * LANES
    row_group = (lax.broadcasted_iota(jnp.int32, (n, LANES), 0) // SSM_GROUP) % per
    lane_group = lax.broadcasted_iota(jnp.int32, (n, LANES), 1) // SSM_GROUP
    taps = jnp.where(row_group == lane_group, _dot(kc_ref[0], rep_c_ref[...]), 0.0).astype(BF16)
    zero = jnp.zeros((LANES, LANES), BF16)
    for s in range(SSM_CHUNK):
        for t in range(SSM_CHUNK):
            lag = t - s
            m_ref[s * LANES:(s + 1) * LANES, t * LANES:(t + 1) * LANES] = (
                taps[lag * LANES:(lag + 1) * LANES] if lag >= 0 else zero)
    wc = wc_ref[0].astype(F32)
    for j in range(per):
        win_ref[:, j * LANES:(j + 1) * LANES] = jnp.where(row_group == j, wc, 0.0).astype(BF16)
    oc = oc_ref[0]
    width = 2 * LANES
    state_group = lax.broadcasted_iota(jnp.int32, (oc.shape[0], width), 0) // (2 * SSM_STATE)
    out_group = (lax.broadcasted_iota(jnp.int32, (oc.shape[0], width), 1) // SSM_GROUP) % per
    for nb in range(n // width):
        cols = slice(nb * width, (nb + 1) * width)
        wout_ref[:, cols] = jnp.where(state_group == out_group, _dot(oc, rep_tc_ref[:, cols]), 0.0).astype(BF16)


def _s5_body(u_ref, kc_ref, wc_ref, oc_ref, rep_c_ref, rep_tc_ref, a1_ref, a2_ref, d_ref, y_ref,
             carry_ref, m_ref, win_ref, wout_ref, *, n_steps):
    @pl.when((pl.program_id(1) == 0) & (pl.program_id(2) == 0))
    def _():
        _s5_expand_operators(kc_ref, wc_ref, oc_ref, rep_c_ref, rep_tc_ref, m_ref, win_ref, wout_ref)

    @pl.when(pl.program_id(2) == 0)
    def _():
        carry_ref[...] = jnp.zeros_like(carry_ref)

    n_rows = u_ref.shape[1] // SSM_CHUNK
    slabs = [u_ref[0, pl.ds(s, n_rows, stride=SSM_CHUNK), :] for s in range(SSM_CHUNK)]
    ucat = jnp.concatenate([slab.astype(BF16) for slab in slabs], axis=1)
    x = _dot(ucat, win_ref[...])
    rows = lax.broadcasted_iota(jnp.int32, (n_rows, LANES), 0)
    x_prev = []
    for i in range(LANES // SSM_GROUP):
        lanes = slice(i * LANES, (i + 1) * LANES)
        a1 = a1_ref[0, :, lanes]
        a2 = a2_ref[0, :, lanes]
        x_in = carry_ref[:, lanes]

        def times_a(v, k):
            return a1[k:k + 1] * v + a2[k:k + 1] * pltpu.roll(v, SSM_STATE, axis=1)

        xi = x[:, lanes] + jnp.where(rows == 0, times_a(x_in, 0)[0:1], 0.0)
        for k in range(n_steps):
            xi = xi + times_a(jnp.where(rows >= (1 << k), pltpu.roll(xi, 1 << k, axis=0), 0.0), k)
        x_prev.append(jnp.where(rows == 0, x_in[0:1], pltpu.roll(xi, 1, axis=0)))
        carry_ref[:, lanes] = jnp.broadcast_to(xi[n_rows - 1:n_rows], (SUBLANES, LANES))
    x_prev = jnp.concatenate(x_prev, axis=1).astype(BF16)
    width = 2 * LANES
    for nb in range(SSM_CHUNK * LANES // width):
        cols = slice(nb * width, (nb + 1) * width)
        hi = (nb + 1) * width
        y = _dot(ucat[:, :hi], m_ref[:hi, cols]) + _dot(x_prev, wout_ref[:, cols])
        for t in range(2 * nb, 2 * nb + 2):
            yt = y[:, (t - 2 * nb) * LANES:(t - 2 * nb + 1) * LANES] + d_ref[0] * slabs[t]
            y_ref[0, pl.ds(t, n_rows, stride=SSM_CHUNK), :] = _gelu_tanh(yt)


def _s5(u, kc, wc, oc, a1, a2, d_t):
    bsz, seq, width = u.shape
    n_blocks = width // LANES
    tile = min(SSM_TILE, seq)
    n_steps = a1.shape[1]
    assert (1 << n_steps) == tile // SSM_CHUNK and seq % tile == 0
    n_in = SSM_CHUNK * LANES
    n_state = (LANES // SSM_GROUP) * 2 * SSM_STATE
    chan = np.arange(LANES) % SSM_GROUP
    rep_c = jnp.asarray(np.arange(SSM_GROUP)[:, None] == chan[None, :], BF16)
    tc_in = np.arange(SSM_CHUNK * SSM_GROUP)
    tc_out = (np.arange(n_in) // LANES) * SSM_GROUP + np.arange(n_in) % SSM_GROUP
    rep_tc = jnp.asarray(tc_in[:, None] == tc_out[None, :], BF16)
    body = functools.partial(_s5_body, n_steps=n_steps)
    per_block = lambda a: pl.BlockSpec((1,) + a.shape[1:], lambda v, b, i: (v, 0, 0))
    const = lambda a: pl.BlockSpec(a.shape, lambda v, b, i: (0, 0))
    tokens = pl.BlockSpec((1, tile, LANES), lambda v, b, i: (b, i, v))
    return pl.pallas_call(
        body,
        name="s5",
        grid=(n_blocks, bsz, seq // tile),
        in_specs=[tokens, per_block(kc), per_block(wc), per_block(oc), const(rep_c), const(rep_tc),
                  per_block(a1), per_block(a2), per_block(d_t)],
        out_specs=tokens,
        out_shape=jax.ShapeDtypeStruct(u.shape, F32),
        scratch_shapes=[pltpu.VMEM((SUBLANES, n_state), F32), pltpu.VMEM((n_in, n_in), BF16),
                        pltpu.VMEM((n_in, n_state), BF16), pltpu.VMEM((n_state, n_in), BF16)],
        compiler_params=pltpu.CompilerParams(
            dimension_semantics=("arbitrary", "arbitrary", "arbitrary"), vmem_limit_bytes=VMEM_LIMIT),
    )(u, kc, wc, oc, rep_c, rep_tc, a1, a2, d_t)


def _ssm_operators(a_re, a_im, log_dt, b_re, b_im, c_re, c_im, d_skip, n_steps):
    t_len = SSM_CHUNK
    groups, n_state = a_re.shape
    per = LANES // SSM_GROUP
    nv = groups // per
    hi = lax.Precision.HIGHEST
    dt = jnp.exp(log_dt)[:, None]
    mag = jnp.exp(a_re * dt)
    lb_re = mag * jnp.cos(a_im * dt)
    lb_im = mag * jnp.sin(a_im * dt)
    nr, ni = lb_re - 1.0, lb_im
    den = a_re * a_re + a_im * a_im
    cr = (nr * a_re + ni * a_im) / den
    ci = (ni * a_re - nr * a_im) / den
    bb_re = cr[..., None] * b_re - ci[..., None] * b_im
    bb_im = cr[..., None] * b_im + ci[..., None] * b_re

    def lam_pow(n):
        n = n.astype(F32)[:, None, None]
        m = jnp.exp(a_re * dt * n)
        return m * jnp.cos(a_im * dt * n), m * jnp.sin(a_im * dt * n)

    pw_re, pw_im = lam_pow(jnp.arange(t_len + 1))
    lbr = pw_re[..., None] * bb_re[None] - pw_im[..., None] * bb_im[None]
    lbi = pw_re[..., None] * bb_im[None] + pw_im[..., None] * bb_re[None]
    taps = (jnp.einsum('gcp,tgpd->tgcd', c_re, lbr[:t_len], precision=hi)
            - jnp.einsum('gcp,tgpd->tgcd', c_im, lbi[:t_len], precision=hi))
    kc = taps.reshape(t_len, nv, per, SSM_GROUP, SSM_GROUP).transpose(1, 0, 2, 4, 3)
    kc = kc.reshape(nv, t_len * LANES, SSM_GROUP).astype(BF16)
    lb = jnp.stack([lbr[:t_len][::-1], lbi[:t_len][::-1]], axis=0)
    lb = lb.reshape(2, t_len, nv, per, n_state, SSM_GROUP).transpose(2, 1, 3, 5, 0, 4)
    wc = lb.reshape(nv, t_len * LANES, 2 * n_state).astype(BF16)
    p_re = pw_re[1:, :, None, :]
    p_im = pw_im[1:, :, None, :]
    clr = c_re[None] * p_re - c_im[None] * p_im
    cli = c_re[None] * p_im + c_im[None] * p_re
    cl = jnp.stack([clr, -cli], axis=0)
    cl = cl.reshape(2, t_len, nv, per, SSM_GROUP, n_state).transpose(2, 3, 0, 5, 1, 4)
    oc = cl.reshape(nv, per * 2 * n_state, t_len * SSM_GROUP).astype(BF16)
    pa_re, pa_im = lam_pow(t_len * (2 ** jnp.arange(n_steps)))
    by_block = lambda a: a.reshape(n_steps, nv, per * 2 * n_state).transpose(1, 0, 2)
    a1 = by_block(jnp.concatenate([pa_re, pa_re], axis=-1))
    a2 = by_block(jnp.concatenate([-pa_im, pa_im], axis=-1))
    d_t = d_skip.reshape(nv, 1, LANES)
    return kc, wc, oc, a1, a2, d_t


def _combine_body(o_ref, gy_ref, sg_ref, x_ref, wap_ref, wglu_ref, bglu_ref, wsp_ref, wout_ref, g2_ref,
                  h_ref, hn_ref):
    attn = _dot(o_ref[0], wap_ref[...])
    g = gy_ref[0]
    s = g * jax.nn.sigmoid(_dot(g.astype(BF16), wglu_ref[...]) + bglu_ref[...])
    ssm = _dot(s.astype(BF16), wsp_ref[...])
    sg = sg_ref[0]
    d_model = attn.shape[1]
    mix = sg[:, :d_model] * attn + sg[:, d_model:] * ssm
    h = x_ref[0] + _dot(mix.astype(BF16), wout_ref[...])
    h_ref[0] = h
    hn_ref[0] = (h * _rms_scale(h) * g2_ref[...]).astype(BF16)


def _combine(o, gy, sg, x, wap, wglu, bglu, wsp, wout, g2):
    bsz, seq, d_model = x.shape
    tm = TOKEN_TILE
    d_u = gy.shape[-1]
    tile = lambda width: pl.BlockSpec((1, tm, width), lambda b, i: (b, i, 0))
    return pl.pallas_call(
        _combine_body,
        name="combine",
        grid=(bsz, seq // tm),
        in_specs=[tile(o.shape[-1]), tile(d_u), tile(2 * d_model), tile(d_model),
                  _resident(wap.shape), _resident(wglu.shape), _resident(bglu.shape),
                  _resident(wsp.shape), _resident(wout.shape), _resident(g2.shape)],
        out_specs=[tile(d_model), tile(d_model)],
        out_shape=[jax.ShapeDtypeStruct(x.shape, F32), jax.ShapeDtypeStruct(x.shape, BF16)],
        compiler_params=pltpu.CompilerParams(
            dimension_semantics=("arbitrary", "arbitrary"), vmem_limit_bytes=VMEM_LIMIT),
    )(o, gy, sg, x, wap, wglu, bglu, wsp, wout, g2)


def _ffn_body(hn_ref, h_ref, wup_ref, cw_ref, cb_ref, wdn_ref, y_ref, carry_ref, act_ref, upg_ref, upv_ref,
              *, d_ff):
    @pl.when(pl.program_id(1) == 0)
    def _():
        carry_ref[...] = jnp.zeros_like(carry_ref)

    hn = hn_ref[0]
    tm = hn.shape[0]

    def conv_cols(col0, up_ref):
        cols = pl.ds(col0, FFN_CHUNK)
        up = _dot(hn, wup_ref[:, cols])
        up_ref[0:SUBLANES, :] = carry_ref[:, cols]
        up_ref[SUBLANES:, :] = up
        carry_ref[:, cols] = up[tm - SUBLANES:, :]
        out = cb_ref[:, cols] + up * cw_ref[CONV_WIDTH - 1:CONV_WIDTH, cols]
        for tap in range(CONV_WIDTH - 1):
            lo = SUBLANES - (CONV_WIDTH - 1) + tap
            out = out + up_ref[lo:lo + tm, :] * cw_ref[tap:tap + 1, cols]
        return out

    for j in range(d_ff // FFN_CHUNK):
        gate = conv_cols(j * FFN_CHUNK, upg_ref)
        val = conv_cols(d_ff + j * FFN_CHUNK, upv_ref)
        act_ref[:, j * FFN_CHUNK:(j + 1) * FFN_CHUNK] = (gate * jax.nn.sigmoid(gate) * val).astype(BF16)
    y_ref[0] = h_ref[0] + _dot(act_ref[...], wdn_ref[...])


def _convffn(hn, h, wup, cw, cb, wdn):
    bsz, seq, d_model = h.shape
    tm = FFN_TILE
    d_ff = wdn.shape[0]
    assert d_ff % FFN_CHUNK == 0 and wup.shape[1] == 2 * d_ff
    tile = pl.BlockSpec((1, tm, d_model), lambda b, i: (b, i, 0))
    body = functools.partial(_ffn_body, d_ff=d_ff)
    return pl.pallas_call(
        body,
        name="convffn",
        grid=(bsz, seq // tm),
        in_specs=[tile, tile, _resident(wup.shape), _resident(cw.shape), _resident(cb.shape),
                  _resident(wdn.shape)],
        out_specs=tile,
        out_shape=jax.ShapeDtypeStruct(h.shape, F32),
        scratch_shapes=[pltpu.VMEM((SUBLANES, 2 * d_ff), F32), pltpu.VMEM((tm, d_ff), BF16),
                        pltpu.VMEM((SUBLANES + tm, FFN_CHUNK), F32), pltpu.VMEM((SUBLANES + tm, FFN_CHUNK), F32)],
        compiler_params=pltpu.CompilerParams(
            dimension_semantics=("arbitrary", "arbitrary"), vmem_limit_bytes=VMEM_LIMIT),
    )(hn, h, wup, cw, cb, wdn)


def kernel(x, norm1_gain, w_in, q_norm_gain, k_norm_gain, lambda_q1, lambda_k1, lambda_q2, lambda_k2,
           subln_gain, w_attn_proj, ssm_a_re, ssm_a_im, ssm_log_dt, ssm_b_re, ssm_b_im, ssm_c_re, ssm_c_im,
           ssm_d, w_glu, b_glu, w_ssm_proj, w_out, norm2_gain, w_up, conv_w, conv_b, w_down):
    bsz, seq, d_model = x.shape
    assert norm1_gain.shape[0] == 1, "single-layer trunk"
    assert seq % (TOKEN_TILE) == 0 and seq % SSM_CHUNK == 0
    d_qk = N_HEADS * HEAD_COLS

    w_in_b = _to_bf16(w_in[0], qk_cols=2 * d_qk)
    d_of_lane = _head_lane_dim()
    lane_gain = lambda g: jnp.tile(g[d_of_lane], N_HEADS)[None, :]
    gq = lane_gain(q_norm_gain[0].astype(F32)) * (LOG2E / math.sqrt(HEAD_DIM))
    gk = lane_gain(k_norm_gain[0].astype(F32))

    pos = np.arange(seq, dtype=np.float32)
    inv_freq = (1.0 / (np.float32(ROPE_THETA) ** (np.arange(0, HEAD_DIM, 2, dtype=np.float32) / np.float32(HEAD_DIM))))
    ang = pos[:, None] * inv_freq.astype(np.float32)[None, :]
    cos_t = jnp.asarray(np.tile(np.cos(ang), (1, HEAD_COLS // ROPE_HALF)), F32)
    sin_h = np.sin(ang)
    sin_t = jnp.asarray(np.concatenate([-sin_h, -sin_h, sin_h, sin_h], axis=1), F32)

    comp = np.tile(_head_lane_component(), N_HEADS)
    head = np.repeat(np.arange(N_HEADS), HEAD_COLS)
    e_np = np.zeros((d_qk, LANES), np.float32)
    e_np[np.arange(d_qk), head * 2 + comp] = 1.0
    e_mat = jnp.asarray(e_np, BF16)
    et_mat = jnp.asarray(np.concatenate([e_np.T, e_np.T], axis=0), BF16)

    qT, k, vT, u, sg = _inproj(x, norm1_gain.astype(F32), w_in_b, gq, gk, cos_t, sin_t, e_mat, et_mat)

    lam = (jnp.exp(jnp.sum(lambda_q1[0].astype(F32) * lambda_k1[0].astype(F32)))
           - jnp.exp(jnp.sum(lambda_q2[0].astype(F32) * lambda_k2[0].astype(F32))) + LAM_INIT)
    score_bound = HEAD_DIM * jnp.max(jnp.abs(gq)) * jnp.max(jnp.abs(gk))
    o = lax.cond(score_bound <= SCORE_LOG2_LIMIT,
                 functools.partial(_attention, bounded=True),
                 functools.partial(_attention, bounded=False),
                 lam.reshape(1, 1), qT, k, vT, subln_gain[0].astype(F32)[:, None])

    n_steps = int(math.log2(min(SSM_TILE, seq) // SSM_CHUNK))
    ops = _ssm_operators(ssm_a_re[0].astype(F32), ssm_a_im[0].astype(F32), ssm_log_dt[0].astype(F32),
                         ssm_b_re[0].astype(F32), ssm_b_im[0].astype(F32), ssm_c_re[0].astype(F32),
                         ssm_c_im[0].astype(F32), ssm_d[0].astype(F32), n_steps)
    gy = _s5(u, *ops)

    h, hn = _combine(o, gy, sg, x, _to_bf16(w_attn_proj[0]), _to_bf16(w_glu[0]),
                     b_glu.astype(F32), _to_bf16(w_ssm_proj[0]), _to_bf16(w_out[0]),
                     norm2_gain.astype(F32))
    return _convffn(hn, h, _to_bf16(w_up[0]), conv_w[0].astype(F32), conv_b.astype(F32),
                    _to_bf16(w_down[0]))
```

```python
import functools
import math

import numpy as np
import jax
import jax.numpy as jnp
from jax import lax
from jax.experimental import pallas as pl
from jax.experimental.pallas import tpu as pltpu

F32 = jnp.float32
BF16 = jnp.bfloat16

N_HEADS = 8
HEAD_DIM = 64
HEAD_COLS = 2 * HEAD_DIM
ROPE_HALF = HEAD_DIM // 2
V_DIM = 128
ROPE_THETA = 10000.0
SSM_GROUP = 16
SSM_GROUPS = 32
SSM_STATE = 64
SSM_CHUNK = 16
CONV_WIDTH = 3
EPS = 1e-6
LAM_INIT = 0.8 - 0.6 * math.exp(-0.3 * 0)

LANES = 128
SUBLANES = 8
TOKEN_TILE = 512
FFN_CHUNK = 256
FFN_TILE = 512
CAST_ROWS = 256
ATTN_UNROLL = 4
SSM_TILE = 8192
VMEM_LIMIT = 56 * 1024 * 1024
MASK_VALUE = -1e30
SCORE_LOG2_LIMIT = 40.0
LOG2E = math.log2(math.e)


def _dot(a, b):
    return jnp.dot(a, b, preferred_element_type=F32)


def _rms_scale(x):
    return lax.rsqrt(jnp.mean(x * x, axis=-1, keepdims=True) + EPS)


def _resident(shape):
    return pl.BlockSpec(shape, lambda *_: (0,) * len(shape), pipeline_mode=pl.Buffered(1))


def _to_bf16_body(w_ref, o_ref, *, qk_cols):
    w = w_ref[...]
    if qk_cols:
        head = w[:, :qk_cols]
        quarter = (lax.broadcasted_iota(jnp.int32, head.shape, 1) % HEAD_COLS) // ROPE_HALF
        from_right = pltpu.roll(head, qk_cols - ROPE_HALF, axis=1)
        from_left = pltpu.roll(head, ROPE_HALF, axis=1)
        head = jnp.where(quarter == 1, from_right, jnp.where(quarter == 2, from_left, head))
        o_ref[:, :qk_cols] = head.astype(BF16)
        o_ref[:, qk_cols:] = w[:, qk_cols:].astype(BF16)
    else:
        o_ref[...] = w.astype(BF16)


def _to_bf16(w, qk_cols=0):
    rows, cols = w.shape
    blk = min(rows, CAST_ROWS)
    assert rows % blk == 0
    return pl.pallas_call(
        functools.partial(_to_bf16_body, qk_cols=qk_cols),
        name="to_bf16",
        grid=(rows // blk,),
        in_specs=[pl.BlockSpec((blk, cols), lambda i: (i, 0))],
        out_specs=pl.BlockSpec((blk, cols), lambda i: (i, 0)),
        out_shape=jax.ShapeDtypeStruct(w.shape, BF16),
        compiler_params=pltpu.CompilerParams(dimension_semantics=("arbitrary",), vmem_limit_bytes=VMEM_LIMIT),
    )(w)


def _head_lane_component():
    j = np.arange(HEAD_COLS)
    return (j // ROPE_HALF) % 2


def _head_lane_dim():
    j = np.arange(HEAD_COLS)
    return (j // HEAD_DIM) * ROPE_HALF + j % ROPE_HALF


def _inproj_body(x_ref, g1_ref, w_ref, gq_ref, gk_ref, cos_ref, sin_ref, e_ref, et_ref,
                 qT_ref, k_ref, vT_ref, u_ref, sg_ref, *, d_qk, d_v, d_u):
    x = x_ref[0]
    xn = (x * _rms_scale(x) * g1_ref[...]).astype(BF16)
    cos = cos_ref[...]
    sin = sin_ref[...]

    def norm_rope(col0, gain_ref):
        t = _dot(xn, w_ref[:, col0:col0 + d_qk])
        ss = _dot((t * t).astype(BF16), e_ref[...])
        r = lax.rsqrt(ss * (1.0 / HEAD_DIM) + EPS)
        r_hi = r.astype(BF16)
        r_lo = (r - r_hi.astype(F32)).astype(BF16)
        rf = _dot(jnp.concatenate([r_hi, r_lo], axis=1), et_ref[...])
        t = t * rf * gain_ref[...]
        heads = []
        for h in range(N_HEADS):
            th = t[:, h * HEAD_COLS:(h + 1) * HEAD_COLS]
            heads.append(th * cos + pltpu.roll(th, HEAD_DIM, axis=1) * sin)
        return jnp.concatenate(heads, axis=1)

    q = norm_rope(0, gq_ref)
    qT_ref[0, 0] = q.T.astype(BF16)
    k = norm_rope(d_qk, gk_ref)
    k_ref[0] = k.astype(BF16)
    c0 = 2 * d_qk
    v = _dot(xn, w_ref[:, c0:c0 + d_v])
    vT_ref[0, 0] = v.T.astype(BF16)
    c0 += d_v
    u_ref[0] = _dot(xn, w_ref[:, c0:c0 + d_u])
    c0 += d_u
    sg_ref[0] = jax.nn.sigmoid(_dot(xn, w_ref[:, c0:]))


def _inproj(x, g1, w_in_b, gq, gk, cos_t, sin_t, e_mat, et_mat):
    bsz, seq, d_model = x.shape
    tm = TOKEN_TILE
    nt = seq // tm
    d_qk = N_HEADS * HEAD_COLS
    d_v = N_HEADS * V_DIM
    d_u = SSM_GROUPS * SSM_GROUP
    d_g = 2 * d_model
    in_cols = w_in_b.shape[1]
    assert in_cols == 2 * d_qk + d_v + d_u + d_g and seq % tm == 0
    body = functools.partial(_inproj_body, d_qk=d_qk, d_v=d_v, d_u=d_u)
    return pl.pallas_call(
        body,
        name="inproj",
        grid=(bsz, nt),
        in_specs=[
            pl.BlockSpec((1, tm, d_model), lambda b, i: (b, i, 0)),
            _resident((1, d_model)),
            _resident((d_model, in_cols)),
            _resident((1, d_qk)),
            _resident((1, d_qk)),
            pl.BlockSpec((tm, LANES), lambda b, i: (i, 0)),
            pl.BlockSpec((tm, LANES), lambda b, i: (i, 0)),
            _resident((d_qk, LANES)),
            _resident((2 * LANES, d_qk)),
        ],
        out_specs=[
            pl.BlockSpec((1, 1, d_qk, tm), lambda b, i: (b, i, 0, 0)),
            pl.BlockSpec((1, tm, d_qk), lambda b, i: (b, i, 0)),
            pl.BlockSpec((1, 1, d_v, tm), lambda b, i: (b, i, 0, 0)),
            pl.BlockSpec((1, tm, d_u), lambda b, i: (b, i, 0)),
            pl.BlockSpec((1, tm, d_g), lambda b, i: (b, i, 0)),
        ],
        out_shape=[
            jax.ShapeDtypeStruct((bsz, nt, d_qk, tm), BF16),
            jax.ShapeDtypeStruct((bsz, seq, d_qk), BF16),
            jax.ShapeDtypeStruct((bsz, nt, d_v, tm), BF16),
            jax.ShapeDtypeStruct((bsz, seq, d_u), F32),
            jax.ShapeDtypeStruct((bsz, seq, d_g), F32),
        ],
        compiler_params=pltpu.CompilerParams(
            dimension_semantics=("arbitrary", "arbitrary"), vmem_limit_bytes=VMEM_LIMIT),
    )(x, g1, w_in_b, gq, gk, cos_t, sin_t, e_mat, et_mat)


def _split_components(qT):
    row = lax.broadcasted_iota(jnp.int32, qT.shape, 0)
    is_c0 = ((row // ROPE_HALF) % 2) == 0
    zero = jnp.zeros_like(qT)
    return jnp.where(is_c0, qT, zero), jnp.where(is_c0, zero, qT)


def _causal(shape):
    return lax.broadcasted_iota(jnp.int32, shape, 0) <= lax.broadcasted_iota(jnp.int32, shape, 1)


def _attn_finish(lam_ref, gsub_ref, acc0, acc1, l0, l1):
    lam = lam_ref[0, 0]
    a = acc0 * (1.0 / l0) - lam * (acc1 * (1.0 / l1))
    a = a * lax.rsqrt(jnp.mean(a * a, axis=0, keepdims=True) + EPS)
    a = a * gsub_ref[...] * (1.0 - LAM_INIT)
    return a.T.astype(BF16)


def _attn_online_body(lam_ref, qT_ref, k_ref, vT_ref, gsub_ref, o_ref, acc0_ref, acc1_ref, *, blk):
    qi = pl.program_id(2)
    q_comp = _split_components(qT_ref[0, 0])
    accs = (acc0_ref, acc1_ref)
    acc0_ref[...] = jnp.zeros_like(acc0_ref)
    acc1_ref[...] = jnp.zeros_like(acc1_ref)

    def block(j, carry, masked):
        kj = k_ref[0, pl.ds(pl.multiple_of(j * blk, blk), blk), :]
        vj = vT_ref[0, j]
        out = []
        for c in range(2):
            m, l = carry[c]
            sT = _dot(kj, q_comp[c])
            if masked:
                sT = jnp.where(_causal(sT.shape), sT, MASK_VALUE)
            m_new = jnp.maximum(m, jnp.max(sT, axis=0, keepdims=True))
            alpha = jnp.exp2(m - m_new)
            p = jnp.exp2(sT - m_new)
            out.append((m_new, alpha * l + jnp.sum(p, axis=0, keepdims=True)))
            accs[c][...] = accs[c][...] * alpha + _dot(vj, p.astype(BF16))
        return tuple(out)

    init = ((jnp.full((1, blk), MASK_VALUE, F32), jnp.zeros((1, blk), F32)),) * 2
    carry = lax.fori_loop(0, qi, lambda j, c: block(j, c, False), init)
    (_, l0), (_, l1) = block(qi, carry, True)
    o_ref[0] = _attn_finish(lam_ref, gsub_ref, acc0_ref[...], acc1_ref[...], l0, l1)


def _attn_bounded_body(lam_ref, qT_ref, k_ref, vT_ref, gsub_ref, o_ref, acc_ref, sa_ref, sb_ref, *, blk):
    qi = pl.program_id(2)
    heads = tuple(slice(s * HEAD_COLS, (s + 1) * HEAD_COLS) for s in range(2))
    q_comp = [_split_components(qT_ref[0, 0, hs, :]) for hs in heads]
    acc_ref[...] = jnp.zeros_like(acc_ref)
    bufs = (sa_ref, sb_ref)

    def scores(s, j):
        kj = k_ref[0, pl.ds(pl.multiple_of(j * blk, blk), blk), heads[s]]
        for c in range(2):
            bufs[s][c] = _dot(kj, q_comp[s][c])

    def consume(s, j, carry, masked):
        vj = vT_ref[0, j, heads[s], :]
        parts = list(carry)
        for c in range(2):
            p = jnp.exp2(bufs[s][c])
            if masked:
                p = jnp.where(_causal(p.shape), p, 0.0)
            parts[2 * s + c] = parts[2 * s + c] + jnp.sum(p.reshape(blk // SUBLANES, SUBLANES, blk), axis=0)
            acc_ref[s, c] += _dot(vj, p.astype(BF16))
        return tuple(parts)

    def run(first, count, carry, diagonal_last):
        n_units = 2 * count
        for u in range(n_units):
            if u + 1 < n_units or not diagonal_last:
                scores((u + 1) % 2, first + (u + 1) // 2)
            carry = consume(u % 2, first + u // 2, carry, diagonal_last and u // 2 == count - 1)
        return carry

    scores(0, 0)
    trips = qi // ATTN_UNROLL
    carry = lax.fori_loop(0, trips, lambda t, c: run(ATTN_UNROLL * t, ATTN_UNROLL, c, False),
                          (jnp.zeros((SUBLANES, blk), F32),) * 4)
    first = trips * ATTN_UNROLL
    tails = [functools.partial(run, first, rem + 1, diagonal_last=True) for rem in range(ATTN_UNROLL)]
    carry = lax.switch(qi - first, tails, carry)
    l = [jnp.sum(part, axis=0, keepdims=True) for part in carry]
    for s in range(2):
        o_ref[0, :, heads[s]] = _attn_finish(lam_ref, gsub_ref, acc_ref[s, 0], acc_ref[s, 1], l[2 * s], l[2 * s + 1])


def _attention(lam, qT, k, vT, gsub, *, bounded):
    bsz, nblk, d_qk, blk = qT.shape
    seq = k.shape[1]
    if bounded:
        group = 2
        body = functools.partial(_attn_bounded_body, blk=blk)
        scratch = [pltpu.VMEM((group, 2, V_DIM, blk), F32),
                   pltpu.VMEM((2, blk, blk), F32), pltpu.VMEM((2, blk, blk), F32)]
    else:
        group = 1
        body = functools.partial(_attn_online_body, blk=blk)
        scratch = [pltpu.VMEM((V_DIM, blk), F32), pltpu.VMEM((V_DIM, blk), F32)]
    return pl.pallas_call(
        body,
        name="attention_bounded" if bounded else "attention_online",
        grid=(bsz, N_HEADS // group, nblk),
        in_specs=[
            pl.BlockSpec(memory_space=pltpu.SMEM),
            pl.BlockSpec((1, 1, group * HEAD_COLS, blk), lambda b, h, i: (b, i, h, 0)),
            pl.BlockSpec((1, seq, group * HEAD_COLS), lambda b, h, i: (b, 0, h)),
            pl.BlockSpec((1, nblk, group * V_DIM, blk), lambda b, h, i: (b, 0, h, 0)),
            pl.BlockSpec((V_DIM, 1), lambda b, h, i: (0, 0)),
        ],
        out_specs=pl.BlockSpec((1, blk, group * V_DIM), lambda b, h, i: (b, i, h)),
        out_shape=jax.ShapeDtypeStruct((bsz, seq, N_HEADS * V_DIM), BF16),
        scratch_shapes=scratch,
        compiler_params=pltpu.CompilerParams(
            dimension_semantics=("arbitrary", "arbitrary", "arbitrary"), vmem_limit_bytes=VMEM_LIMIT),
    )(lam, qT, k, vT, gsub)


def _gelu_tanh(y):
    return 0.5 * y * (1.0 + jnp.tanh(math.sqrt(2.0 / math.pi) * (y + 0.044715 * (y * y * y))))


def _s5_expand_operators(kc_ref, wc_ref, oc_ref, rep_c_ref, rep_tc_ref, m_ref, win_ref, wout_ref):
    per = LANES // SSM_GROUP
    n = SSM_CHUNK * LANES
    row_group = (lax.broadcasted_iota(jnp.int32, (n, LANES), 0) // SSM_GROUP) % per
    lane_group = lax.broadcasted_iota(jnp.int32, (n, LANES), 1) // SSM_GROUP
    taps = jnp.where(row_group == lane_group, _dot(kc_ref[0], rep_c_ref[...]), 0.0).astype(BF16)
    zero = jnp.zeros((LANES, LANES), BF16)
    for s in range(SSM_CHUNK):
        for t in range(SSM_CHUNK):
            lag = t - s
            m_ref[s * LANES:(s + 1) * LANES, t * LANES:(t + 1) * LANES] = (
                taps[lag * LANES:(lag + 1) * LANES] if lag >= 0 else zero)
    wc = wc_ref[0].astype(F32)
    for j in range(per):
        win_ref[:, j * LANES:(j + 1) * LANES] = jnp.where(row_group == j, wc, 0.0).astype(BF16)
    oc = oc_ref[0]
    width = 2 * LANES
    state_group = lax.broadcasted_iota(jnp.int32, (oc.shape[0], width), 0) // (2 * SSM_STATE)
    out_group = (lax.broadcasted_iota(jnp.int32, (oc.shape[0], width), 1) // SSM_GROUP) % per
    for nb in range(n // width):
        cols = slice(nb * width, (nb + 1) * width)
        wout_ref[:, cols] = jnp.where(state_group == out_group, _dot(oc, rep_tc_ref[:, cols]), 0.0).astype(BF16)


def _s5_body(u_ref, kc_ref, wc_ref, oc_ref, rep_c_ref, rep_tc_ref, a1_ref, a2_ref, d_ref, y_ref,
             carry_ref, m_ref, win_ref, wout_ref, *, n_steps):
    @pl.when((pl.program_id(1) == 0) & (pl.program_id(2) == 0))
    def _():
        _s5_expand_operators(kc_ref, wc_ref, oc_ref, rep_c_ref, rep_tc_ref, m_ref, win_ref, wout_ref)

    @pl.when(pl.program_id(2) == 0)
    def _():
        carry_ref[...] = jnp.zeros_like(carry_ref)

    n_rows = u_ref.shape[1] // SSM_CHUNK
    slabs = [u_ref[0, pl.ds(s, n_rows, stride=SSM_CHUNK), :] for s in range(SSM_CHUNK)]
    ucat = jnp.concatenate([slab.astype(BF16) for slab in slabs], axis=1)
    x = _dot(ucat, win_ref[...])
    rows = lax.broadcasted_iota(jnp.int32, (n_rows, LANES), 0)
    x_prev = []
    for i in range(LANES // SSM_GROUP):
        lanes = slice(i * LANES, (i + 1) * LANES)
        a1 = a1_ref[0, :, lanes]
        a2 = a2_ref[0, :, lanes]
        x_in = carry_ref[:, lanes]

        def times_a(v, k):
            return a1[k:k + 1] * v + a2[k:k + 1] * pltpu.roll(v, SSM_STATE, axis=1)

        xi = x[:, lanes] + jnp.where(rows == 0, times_a(x_in, 0)[0:1], 0.0)
        for k in range(n_steps):
            xi = xi + times_a(jnp.where(rows >= (1 << k), pltpu.roll(xi, 1 << k, axis=0), 0.0), k)
        x_prev.append(jnp.where(rows == 0, x_in[0:1], pltpu.roll(xi, 1, axis=0)))
        carry_ref[:, lanes] = jnp.broadcast_to(xi[n_rows - 1:n_rows], (SUBLANES, LANES))
    x_prev = jnp.concatenate(x_prev, axis=1).astype(BF16)
    width = 2 * LANES
    for nb in range(SSM_CHUNK * LANES // width):
        cols = slice(nb * width, (nb + 1) * width)
        hi = (nb + 1) * width
        y = _dot(ucat[:, :hi], m_ref[:hi, cols]) + _dot(x_prev, wout_ref[:, cols])
        for t in range(2 * nb, 2 * nb + 2):
            yt = y[:, (t - 2 * nb) * LANES:(t - 2 * nb + 1) * LANES] + d_ref[0] * slabs[t]
            y_ref[0, pl.ds(t, n_rows, stride=SSM_CHUNK), :] = _gelu_tanh(yt)


def _s5(u, kc, wc, oc, a1, a2, d_t):
    bsz, seq, width = u.shape
    n_blocks = width // LANES
    tile = min(SSM_TILE, seq)
    n_steps = a1.shape[1]
    assert (1 << n_steps) == tile // SSM_CHUNK and seq % tile == 0
    n_in = SSM_CHUNK * LANES
    n_state = (LANES // SSM_GROUP) * 2 * SSM_STATE
    chan = np.arange(LANES) % SSM_GROUP
    rep_c = jnp.asarray(np.arange(SSM_GROUP)[:, None] == chan[None, :], BF16)
    tc_in = np.arange(SSM_CHUNK * SSM_GROUP)
    tc_out = (np.arange(n_in) // LANES) * SSM_GROUP + np.arange(n_in) % SSM_GROUP
    rep_tc = jnp.asarray(tc_in[:, None] == tc_out[None, :], BF16)
    body = functools.partial(_s5_body, n_steps=n_steps)
    per_block = lambda a: pl.BlockSpec((1,) + a.shape[1:], lambda v, b, i: (v, 0, 0))
    const = lambda a: pl.BlockSpec(a.shape, lambda v, b, i: (0, 0))
    tokens = pl.BlockSpec((1, tile, LANES), lambda v, b, i: (b, i, v))
    return pl.pallas_call(
        body,
        name="s5",
        grid=(n_blocks, bsz, seq // tile),
        in_specs=[tokens, per_block(kc), per_block(wc), per_block(oc), const(rep_c), const(rep_tc),
                  per_block(a1), per_block(a2), per_block(d_t)],
        out_specs=tokens,
        out_shape=jax.ShapeDtypeStruct(u.shape, F32),
        scratch_shapes=[pltpu.VMEM((SUBLANES, n_state), F32), pltpu.VMEM((n_in, n_in), BF16),
                        pltpu.VMEM((n_in, n_state), BF16), pltpu.VMEM((n_state, n_in), BF16)],
        compiler_params=pltpu.CompilerParams(
            dimension_semantics=("arbitrary", "arbitrary", "arbitrary"), vmem_limit_bytes=VMEM_LIMIT),
    )(u, kc, wc, oc, rep_c, rep_tc, a1, a2, d_t)


def _ssm_operators(a_re, a_im, log_dt, b_re, b_im, c_re, c_im, d_skip, n_steps):
    t_len = SSM_CHUNK
    groups, n_state = a_re.shape
    per = LANES // SSM_GROUP
    nv = groups // per
    hi = lax.Precision.HIGHEST
    dt = jnp.exp(log_dt)[:, None]
    mag = jnp.exp(a_re * dt)
    lb_re = mag * jnp.cos(a_im * dt)
    lb_im = mag * jnp.sin(a_im * dt)
    nr, ni = lb_re - 1.0, lb_im
    den = a_re * a_re + a_im * a_im
    cr = (nr * a_re + ni * a_im) / den
    ci = (ni * a_re - nr * a_im) / den
    bb_re = cr[..., None] * b_re - ci[..., None] * b_im
    bb_im = cr[..., None] * b_im + ci[..., None] * b_re

    def lam_pow(n):
        n = n.astype(F32)[:, None, None]
        m = jnp.exp(a_re * dt * n)
        return m * jnp.cos(a_im * dt * n), m * jnp.sin(a_im * dt * n)

    pw_re, pw_im = lam_pow(jnp.arange(t_len + 1))
    lbr = pw_re[..., None] * bb_re[None] - pw_im[..., None] * bb_im[None]
    lbi = pw_re[..., None] * bb_im[None] + pw_im[..., None] * bb_re[None]
    taps = (jnp.einsum('gcp,tgpd->tgcd', c_re, lbr[:t_len], precision=hi)
            - jnp.einsum('gcp,tgpd->tgcd', c_im, lbi[:t_len], precision=hi))
    kc = taps.reshape(t_len, nv, per, SSM_GROUP, SSM_GROUP).transpose(1, 0, 2, 4, 3)
    kc = kc.reshape(nv, t_len * LANES, SSM_GROUP).astype(BF16)
    lb = jnp.stack([lbr[:t_len][::-1], lbi[:t_len][::-1]], axis=0)
    lb = lb.reshape(2, t_len, nv, per, n_state, SSM_GROUP).transpose(2, 1, 3, 5, 0, 4)
    wc = lb.reshape(nv, t_len * LANES, 2 * n_state).astype(BF16)
    p_re = pw_re[1:, :, None, :]
    p_im = pw_im[1:, :, None, :]
    clr = c_re[None] * p_re - c_im[None] * p_im
    cli = c_re[None] * p_im + c_im[None] * p_re
    cl = jnp.stack([clr, -cli], axis=0)
    cl = cl.reshape(2, t_len, nv, per, SSM_GROUP, n_state).transpose(2, 3, 0, 5, 1, 4)
    oc = cl.reshape(nv, per * 2 * n_state, t_len * SSM_GROUP).astype(BF16)
    pa_re, pa_im = lam_pow(t_len * (2 ** jnp.arange(n_steps)))
    by_block = lambda a: a.reshape(n_steps, nv, per * 2 * n_state).transpose(1, 0, 2)
    a1 = by_block(jnp.concatenate([pa_re, pa_re], axis=-1))
    a2 = by_block(jnp.concatenate([-pa_im, pa_im], axis=-1))
    d_t = d_skip.reshape(nv, 1, LANES)
    return kc, wc, oc, a1, a2, d_t


def _combine_body(o_ref, gy_ref, sg_ref, x_ref, wap_ref, wglu_ref, bglu_ref, wsp_ref, wout_ref, g2_ref,
                  h_ref, hn_ref):
    attn = _dot(o_ref[0], wap_ref[...])
    g = gy_ref[0]
    s = g * jax.nn.sigmoid(_dot(g.astype(BF16), wglu_ref[...]) + bglu_ref[...])
    ssm = _dot(s.astype(BF16), wsp_ref[...])
    sg = sg_ref[0]
    d_model = attn.shape[1]
    mix = sg[:, :d_model] * attn + sg[:, d_model:] * ssm
    h = x_ref[0] + _dot(mix.astype(BF16), wout_ref[...])
    h_ref[0] = h
    hn_ref[0] = (h * _rms_scale(h) * g2_ref[...]).astype(BF16)


def _combine(o, gy, sg, x, wap, wglu, bglu, wsp, wout, g2):
    bsz, seq, d_model = x.shape
    tm = TOKEN_TILE
    d_u = gy.shape[-1]
    tile = lambda width: pl.BlockSpec((1, tm, width), lambda b, i: (b, i, 0))
    return pl.pallas_call(
        _combine_body,
        name="combine",
        grid=(bsz, seq // tm),
        in_specs=[tile(o.shape[-1]), tile(d_u), tile(2 * d_model), tile(d_model),
                  _resident(wap.shape), _resident(wglu.shape), _resident(bglu.shape),
                  _resident(wsp.shape), _resident(wout.shape), _resident(g2.shape)],
        out_specs=[tile(d_model), tile(d_model)],
        out_shape=[jax.ShapeDtypeStruct(x.shape, F32), jax.ShapeDtypeStruct(x.shape, BF16)],
        compiler_params=pltpu.CompilerParams(
            dimension_semantics=("arbitrary", "arbitrary"), vmem_limit_bytes=VMEM_LIMIT),
    )(o, gy, sg, x, wap, wglu, bglu, wsp, wout, g2)


def _ffn_body(hn_ref, h_ref, wup_ref, cw_ref, cb_ref, wdn_ref, y_ref, carry_ref, act_ref, upg_ref, upv_ref,
              *, d_ff):
    @pl.when(pl.program_id(1) == 0)
    def _():
        carry_ref[...] = jnp.zeros_like(carry_ref)

    hn = hn_ref[0]
    tm = hn.shape[0]

    def conv_cols(col0, up_ref):
        cols = pl.ds(col0, FFN_CHUNK)
        up = _dot(hn, wup_ref[:, cols])
        up_ref[0:SUBLANES, :] = carry_ref[:, cols]
        up_ref[SUBLANES:, :] = up
        carry_ref[:, cols] = up[tm - SUBLANES:, :]
        out = cb_ref[:, cols] + up * cw_ref[CONV_WIDTH - 1:CONV_WIDTH, cols]
        for tap in range(CONV_WIDTH - 1):
            lo = SUBLANES - (CONV_WIDTH - 1) + tap
            out = out + up_ref[lo:lo + tm, :] * cw_ref[tap:tap + 1, cols]
        return out

    for j in range(d_ff // FFN_CHUNK):
        gate = conv_cols(j * FFN_CHUNK, upg_ref)
        val = conv_cols(d_ff + j * FFN_CHUNK, upv_ref)
        act_ref[:, j * FFN_CHUNK:(j + 1) * FFN_CHUNK] = (gate * jax.nn.sigmoid(gate) * val).astype(BF16)
    y_ref[0] = h_ref[0] + _dot(act_ref[...], wdn_ref[...])


def _convffn(hn, h, wup, cw, cb, wdn):
    bsz, seq, d_model = h.shape
    tm = FFN_TILE
    d_ff = wdn.shape[0]
    assert d_ff % FFN_CHUNK == 0 and wup.shape[1] == 2 * d_ff
    tile = pl.BlockSpec((1, tm, d_model), lambda b, i: (b, i, 0))
    body = functools.partial(_ffn_body, d_ff=d_ff)
    return pl.pallas_call(
        body,
        name="convffn",
        grid=(bsz, seq // tm),
        in_specs=[tile, tile, _resident(wup.shape), _resident(cw.shape), _resident(cb.shape),
                  _resident(wdn.shape)],
        out_specs=tile,
        out_shape=jax.ShapeDtypeStruct(h.shape, F32),
        scratch_shapes=[pltpu.VMEM((SUBLANES, 2 * d_ff), F32), pltpu.VMEM((tm, d_ff), BF16),
                        pltpu.VMEM((SUBLANES + tm, FFN_CHUNK), F32), pltpu.VMEM((SUBLANES + tm, FFN_CHUNK), F32)],
        compiler_params=pltpu.CompilerParams(
            dimension_semantics=("arbitrary", "arbitrary"), vmem_limit_bytes=VMEM_LIMIT),
    )(hn, h, wup, cw, cb, wdn)


def kernel(x, norm1_gain, w_in, q_norm_gain, k_norm_gain, lambda_q1, lambda_k1, lambda_q2, lambda_k2,
           subln_gain, w_attn_proj, ssm_a_re, ssm_a_im, ssm_log_dt, ssm_b_re, ssm_b_im, ssm_c_re, ssm_c_im,
           ssm_d, w_glu, b_glu, w_ssm_proj, w_out, norm2_gain, w_up, conv_w, conv_b, w_down):
    bsz, seq, d_model = x.shape
    assert norm1_gain.shape[0] == 1, "single-layer trunk"
    assert seq % (TOKEN_TILE) == 0 and seq % SSM_CHUNK == 0
    d_qk = N_HEADS * HEAD_COLS

    w_in_b = _to_bf16(w_in[0], qk_cols=2 * d_qk)
    d_of_lane = _head_lane_dim()
    lane_gain = lambda g: jnp.tile(g[d_of_lane], N_HEADS)[None, :]
    gq = lane_gain(q_norm_gain[0].astype(F32)) * (LOG2E / math.sqrt(HEAD_DIM))
    gk = lane_gain(k_norm_gain[0].astype(F32))

    pos = np.arange(seq, dtype=np.float32)
    inv_freq = (1.0 / (np.float32(ROPE_THETA) ** (np.arange(0, HEAD_DIM, 2, dtype=np.float32) / np.float32(HEAD_DIM))))
    ang = pos[:, None] * inv_freq.astype(np.float32)[None, :]
    cos_t = jnp.asarray(np.tile(np.cos(ang), (1, HEAD_COLS // ROPE_HALF)), F32)
    sin_h = np.sin(ang)
    sin_t = jnp.asarray(np.concatenate([-sin_h, -sin_h, sin_h, sin_h], axis=1), F32)

    comp = np.tile(_head_lane_component(), N_HEADS)
    head = np.repeat(np.arange(N_HEADS), HEAD_COLS)
    e_np = np.zeros((d_qk, LANES), np.float32)
    e_np[np.arange(d_qk), head * 2 + comp] = 1.0
    e_mat = jnp.asarray(e_np, BF16)
    et_mat = jnp.asarray(np.concatenate([e_np.T, e_np.T], axis=0), BF16)

    qT, k, vT, u, sg = _inproj(x, norm1_gain.astype(F32), w_in_b, gq, gk, cos_t, sin_t, e_mat, et_mat)

    lam = (jnp.exp(jnp.sum(lambda_q1[0].astype(F32) * lambda_k1[0].astype(F32)))
           - jnp.exp(jnp.sum(lambda_q2[0].astype(F32) * lambda_k2[0].astype(F32))) + LAM_INIT)
    score_bound = HEAD_DIM * jnp.max(jnp.abs(gq)) * jnp.max(jnp.abs(gk))
    o = lax.cond(score_bound <= SCORE_LOG2_LIMIT,
                 functools.partial(_attention, bounded=True),
                 functools.partial(_attention, bounded=False),
                 lam.reshape(1, 1), qT, k, vT, subln_gain[0].astype(F32)[:, None])

    n_steps = int(math.log2(min(SSM_TILE, seq) // SSM_CHUNK))
    ops = _ssm_operators(ssm_a_re[0].astype(F32), ssm_a_im[0].astype(F32), ssm_log_dt[0].astype(F32),
                         ssm_b_re[0].astype(F32), ssm_b_im[0].astype(F32), ssm_c_re[0].astype(F32),
                         ssm_c_im[0].astype(F32), ssm_d[0].astype(F32), n_steps)
    gy = _s5(u, *ops)

    h, hn = _combine(o, gy, sg, x, _to_bf16(w_attn_proj[0]), _to_bf16(w_glu[0]),
                     b_glu.astype(F32), _to_bf16(w_ssm_proj[0]), _to_bf16(w_out[0]),
                     norm2_gain.astype(F32))
    return _convffn(hn, h, _to_bf16(w_up[0]), conv_w[0].astype(F32), conv_b.astype(F32),
                    _to_bf16(w_down[0]))
```

```python
import functools
import math

import numpy as np
import jax
import jax.numpy as jnp
from jax import lax
from jax.experimental import pallas as pl
from jax.experimental.pallas import tpu as pltpu

F32 = jnp.float32
BF16 = jnp.bfloat16

N_HEADS = 8
HEAD_DIM = 64
HEAD_COLS = 2 * HEAD_DIM
ROPE_HALF = HEAD_DIM // 2
V_DIM = 128
ROPE_THETA = 10000.0
SSM_GROUP = 16
SSM_GROUPS = 32
SSM_STATE = 64
SSM_CHUNK = 16
CONV_WIDTH = 3
EPS = 1e-6
LAM_INIT = 0.8 - 0.6 * math.exp(-0.3 * 0)

LANES = 128
SUBLANES = 8
TOKEN_TILE = 512
FFN_CHUNK = 256
FFN_TILE = 512
CAST_ROWS = 256
ATTN_UNROLL = 4
SSM_TILE = 4096
VMEM_LIMIT = 56 * 1024 * 1024
MASK_VALUE = -1e30
SCORE_LOG2_LIMIT = 40.0
LOG2E = math.log2(math.e)


def _dot(a, b):
    return jnp.dot(a, b, preferred_element_type=F32)


def _rms_scale(x):
    return lax.rsqrt(jnp.mean(x * x, axis=-1, keepdims=True) + EPS)


def _resident(shape):
    return pl.BlockSpec(shape, lambda *_: (0,) * len(shape), pipeline_mode=pl.Buffered(1))


def _to_bf16_body(w_ref, o_ref, *, qk_cols):
    w = w_ref[...]
    if qk_cols:
        head = w[:, :qk_cols]
        quarter = (lax.broadcasted_iota(jnp.int32, head.shape, 1) % HEAD_COLS) // ROPE_HALF
        from_right = pltpu.roll(head, qk_cols - ROPE_HALF, axis=1)
        from_left = pltpu.roll(head, ROPE_HALF, axis=1)
        head = jnp.where(quarter == 1, from_right, jnp.where(quarter == 2, from_left, head))
        o_ref[:, :qk_cols] = head.astype(BF16)
        o_ref[:, qk_cols:] = w[:, qk_cols:].astype(BF16)
    else:
        o_ref[...] = w.astype(BF16)


def _to_bf16(w, qk_cols=0):
    rows, cols = w.shape
    blk = min(rows, CAST_ROWS)
    assert rows % blk == 0
    return pl.pallas_call(
        functools.partial(_to_bf16_body, qk_cols=qk_cols),
        name="to_bf16",
        grid=(rows // blk,),
        in_specs=[pl.BlockSpec((blk, cols), lambda i: (i, 0))],
        out_specs=pl.BlockSpec((blk, cols), lambda i: (i, 0)),
        out_shape=jax.ShapeDtypeStruct(w.shape, BF16),
        compiler_params=pltpu.CompilerParams(dimension_semantics=("arbitrary",), vmem_limit_bytes=VMEM_LIMIT),
    )(w)


def _head_lane_component():
    j = np.arange(HEAD_COLS)
    return (j // ROPE_HALF) % 2


def _head_lane_dim():
    j = np.arange(HEAD_COLS)
    return (j // HEAD_DIM) * ROPE_HALF + j % ROPE_HALF


def _inproj_body(x_ref, g1_ref, w_ref, gq_ref, gk_ref, cos_ref, sin_ref, e_ref, et_ref,
                 qT_ref, k_ref, vT_ref, u_ref, sg_ref, *, d_qk, d_v, d_u):
    x = x_ref[0]
    xn = (x * _rms_scale(x) * g1_ref[...]).astype(BF16)
    cos = cos_ref[...]
    sin = sin_ref[...]

    def norm_rope(col0, gain_ref):
        t = _dot(xn, w_ref[:, col0:col0 + d_qk])
        ss = _dot((t * t).astype(BF16), e_ref[...])
        r = lax.rsqrt(ss * (1.0 / HEAD_DIM) + EPS)
        r_hi = r.astype(BF16)
        r_lo = (r - r_hi.astype(F32)).astype(BF16)
        rf = _dot(jnp.concatenate([r_hi, r_lo], axis=1), et_ref[...])
        t = t * rf * gain_ref[...]
        heads = []
        for h in range(N_HEADS):
            th = t[:, h * HEAD_COLS:(h + 1) * HEAD_COLS]
            heads.append(th * cos + pltpu.roll(th, HEAD_DIM, axis=1) * sin)
        return jnp.concatenate(heads, axis=1)

    q = norm_rope(0, gq_ref)
    qT_ref[0, 0] = q.T.astype(BF16)
    k = norm_rope(d_qk, gk_ref)
    k_ref[0] = k.astype(BF16)
    c0 = 2 * d_qk
    v = _dot(xn, w_ref[:, c0:c0 + d_v])
    vT_ref[0, 0] = v.T.astype(BF16)
    c0 += d_v
    u_ref[0] = _dot(xn, w_ref[:, c0:c0 + d_u])
    c0 += d_u
    sg_ref[0] = jax.nn.sigmoid(_dot(xn, w_ref[:, c0:]))


def _inproj(x, g1, w_in_b, gq, gk, cos_t, sin_t, e_mat, et_mat):
    bsz, seq, d_model = x.shape
    tm = TOKEN_TILE
    nt = seq // tm
    d_qk = N_HEADS * HEAD_COLS
    d_v = N_HEADS * V_DIM
    d_u = SSM_GROUPS * SSM_GROUP
    d_g = 2 * d_model
    in_cols = w_in_b.shape[1]
    assert in_cols == 2 * d_qk + d_v + d_u + d_g and seq % tm == 0
    body = functools.partial(_inproj_body, d_qk=d_qk, d_v=d_v, d_u=d_u)
    return pl.pallas_call(
        body,
        name="inproj",
        grid=(bsz, nt),
        in_specs=[
            pl.BlockSpec((1, tm, d_model), lambda b, i: (b, i, 0)),
            _resident((1, d_model)),
            _resident((d_model, in_cols)),
            _resident((1, d_qk)),
            _resident((1, d_qk)),
            pl.BlockSpec((tm, LANES), lambda b, i: (i, 0)),
            pl.BlockSpec((tm, LANES), lambda b, i: (i, 0)),
            _resident((d_qk, LANES)),
            _resident((2 * LANES, d_qk)),
        ],
        out_specs=[
            pl.BlockSpec((1, 1, d_qk, tm), lambda b, i: (b, i, 0, 0)),
            pl.BlockSpec((1, tm, d_qk), lambda b, i: (b, i, 0)),
            pl.BlockSpec((1, 1, d_v, tm), lambda b, i: (b, i, 0, 0)),
            pl.BlockSpec((1, tm, d_u), lambda b, i: (b, i, 0)),
            pl.BlockSpec((1, tm, d_g), lambda b, i: (b, i, 0)),
        ],
        out_shape=[
            jax.ShapeDtypeStruct((bsz, nt, d_qk, tm), BF16),
            jax.ShapeDtypeStruct((bsz, seq, d_qk), BF16),
            jax.ShapeDtypeStruct((bsz, nt, d_v, tm), BF16),
            jax.ShapeDtypeStruct((bsz, seq, d_u), F32),
            jax.ShapeDtypeStruct((bsz, seq, d_g), F32),
        ],
        compiler_params=pltpu.CompilerParams(
            dimension_semantics=("arbitrary", "arbitrary"), vmem_limit_bytes=VMEM_LIMIT),
    )(x, g1, w_in_b, gq, gk, cos_t, sin_t, e_mat, et_mat)


def _split_components(qT):
    row = lax.broadcasted_iota(jnp.int32, qT.shape, 0)
    is_c0 = ((row // ROPE_HALF) % 2) == 0
    zero = jnp.zeros_like(qT)
    return jnp.where(is_c0, qT, zero), jnp.where(is_c0, zero, qT)


def _causal(shape):
    return lax.broadcasted_iota(jnp.int32, shape, 0) <= lax.broadcasted_iota(jnp.int32, shape, 1)


def _attn_finish(lam_ref, gsub_ref, acc0, acc1, l0, l1):
    lam = lam_ref[0, 0]
    a = acc0 * (1.0 / l0) - lam * (acc1 * (1.0 / l1))
    a = a * lax.rsqrt(jnp.mean(a * a, axis=0, keepdims=True) + EPS)
    a = a * gsub_ref[...] * (1.0 - LAM_INIT)
    return a.T.astype(BF16)


def _attn_online_body(lam_ref, qT_ref, k_ref, vT_ref, gsub_ref, o_ref, acc0_ref, acc1_ref, *, blk):
    qi = pl.program_id(2)
    q_comp = _split_components(qT_ref[0, 0])
    accs = (acc0_ref, acc1_ref)
    acc0_ref[...] = jnp.zeros_like(acc0_ref)
    acc1_ref[...] = jnp.zeros_like(acc1_ref)

    def block(j, carry, masked):
        kj = k_ref[0, pl.ds(pl.multiple_of(j * blk, blk), blk), :]
        vj = vT_ref[0, j]
        out = []
        for c in range(2):
            m, l = carry[c]
            sT = _dot(kj, q_comp[c])
            if masked:
                sT = jnp.where(_causal(sT.shape), sT, MASK_VALUE)
            m_new = jnp.maximum(m, jnp.max(sT, axis=0, keepdims=True))
            alpha = jnp.exp2(m - m_new)
            p = jnp.exp2(sT - m_new)
            out.append((m_new, alpha * l + jnp.sum(p, axis=0, keepdims=True)))
            accs[c][...] = accs[c][...] * alpha + _dot(vj, p.astype(BF16))
        return tuple(out)

    init = ((jnp.full((1, blk), MASK_VALUE, F32), jnp.zeros((1, blk), F32)),) * 2
    carry = lax.fori_loop(0, qi, lambda j, c: block(j, c, False), init)
    (_, l0), (_, l1) = block(qi, carry, True)
    o_ref[0] = _attn_finish(lam_ref, gsub_ref, acc0_ref[...], acc1_ref[...], l0, l1)


def _attn_bounded_body(lam_ref, qT_ref, k_ref, vT_ref, gsub_ref, o_ref, acc_ref, sa_ref, sb_ref, *, blk):
    qi = pl.program_id(2)
    heads = tuple(slice(s * HEAD_COLS, (s + 1) * HEAD_COLS) for s in range(2))
    q_comp = [_split_components(qT_ref[0, 0, hs, :]) for hs in heads]
    acc_ref[...] = jnp.zeros_like(acc_ref)
    bufs = (sa_ref, sb_ref)

    half = blk // 2

    def scores(s, j, diagonal=False):
        kj = k_ref[0, pl.ds(pl.multiple_of(j * blk, blk), blk), heads[s]]
        for c in range(2):
            if diagonal:
                bufs[s][c, 0:half, :] = _dot(kj[0:half], q_comp[s][c])
                bufs[s][c, half:, half:] = _dot(kj[half:], q_comp[s][c][:, half:])
            else:
                bufs[s][c] = _dot(kj, q_comp[s][c])

    def consume(s, j, carry, masked):
        vj = vT_ref[0, j, heads[s], :]
        parts = list(carry)
        sum8 = lambda p: jnp.sum(p.reshape(p.shape[0] // SUBLANES, SUBLANES, p.shape[1]), axis=0)
        for c in range(2):
            if masked:
                p_lo = jnp.exp2(bufs[s][c, 0:half, :])
                p_lo = jnp.where(_causal(p_lo.shape), p_lo, 0.0)
                p_hi = jnp.exp2(bufs[s][c, half:, half:])
                p_hi = jnp.where(_causal(p_hi.shape), p_hi, 0.0)
                parts[2 * s + c] = (parts[2 * s + c] + sum8(p_lo)
                                    + jnp.concatenate([jnp.zeros((SUBLANES, half), F32), sum8(p_hi)], axis=1))
                acc_ref[s, c] += _dot(vj[:, 0:half], p_lo.astype(BF16))
                acc_ref[s, c, :, half:] += _dot(vj[:, half:], p_hi.astype(BF16))
            else:
                p = jnp.exp2(bufs[s][c])
                parts[2 * s + c] = parts[2 * s + c] + sum8(p)
                acc_ref[s, c] += _dot(vj, p.astype(BF16))
        return tuple(parts)

    def run(first, count, carry, diagonal_last):
        n_units = 2 * count
        for u in range(n_units):
            if u + 1 < n_units or not diagonal_last:
                scores((u + 1) % 2, first + (u + 1) // 2, diagonal_last and (u + 1) // 2 == count - 1)
            carry = consume(u % 2, first + u // 2, carry, diagonal_last and u // 2 == count - 1)
        return carry

    scores(0, 0)
    trips = qi // ATTN_UNROLL
    carry = lax.fori_loop(0, trips, lambda t, c: run(ATTN_UNROLL * t, ATTN_UNROLL, c, False),
                          (jnp.zeros((SUBLANES, blk), F32),) * 4)
    first = trips * ATTN_UNROLL
    tails = [functools.partial(run, first, rem + 1, diagonal_last=True) for rem in range(ATTN_UNROLL)]
    carry = lax.switch(qi - first, tails, carry)
    l = [jnp.sum(part, axis=0, keepdims=True) for part in carry]
    for s in range(2):
        o_ref[0, :, heads[s]] = _attn_finish(lam_ref, gsub_ref, acc_ref[s, 0], acc_ref[s, 1], l[2 * s], l[2 * s + 1])


def _attention(lam, qT, k, vT, gsub, *, bounded):
    bsz, nblk, d_qk, blk = qT.shape
    seq = k.shape[1]
    if bounded:
        group = 2
        body = functools.partial(_attn_bounded_body, blk=blk)
        scratch = [pltpu.VMEM((group, 2, V_DIM, blk), F32),
                   pltpu.VMEM((2, blk, blk), F32), pltpu.VMEM((2, blk, blk), F32)]
    else:
        group = 1
        body = functools.partial(_attn_online_body, blk=blk)
        scratch = [pltpu.VMEM((V_DIM, blk), F32), pltpu.VMEM((V_DIM, blk), F32)]
    return pl.pallas_call(
        body,
        name="attention_bounded" if bounded else "attention_online",
        grid=(bsz, N_HEADS // group, nblk),
        in_specs=[
            pl.BlockSpec(memory_space=pltpu.SMEM),
            pl.BlockSpec((1, 1, group * HEAD_COLS, blk), lambda b, h, i: (b, i, h, 0)),
            pl.BlockSpec((1, seq, group * HEAD_COLS), lambda b, h, i: (b, 0, h)),
            pl.BlockSpec((1, nblk, group * V_DIM, blk), lambda b, h, i: (b, 0, h, 0)),
            pl.BlockSpec((V_DIM, 1), lambda b, h, i: (0, 0)),
        ],
        out_specs=pl.BlockSpec((1, blk, group * V_DIM), lambda b, h, i: (b, i, h)),
        out_shape=jax.ShapeDtypeStruct((bsz, seq, N_HEADS * V_DIM), BF16),
        scratch_shapes=scratch,
        compiler_params=pltpu.CompilerParams(
            dimension_semantics=("arbitrary", "arbitrary", "arbitrary"), vmem_limit_bytes=VMEM_LIMIT),
    )(lam, qT, k, vT, gsub)


def _gelu_tanh(y):
    return 0.5 * y * (1.0 + jnp.tanh(math.sqrt(2.0 / math.pi) * (y + 0.044715 * (y * y * y))))


def _s5_expand_operators(kc_ref, wc_ref, oc_ref, rep_c_ref, rep_tc_ref, m_ref, win_ref, wout_ref):
    per = LANES // SSM_GROUP
    n = SSM_CHUNK * LANES
    row_group = (lax.broadcasted_iota(jnp.int32, (n, LANES), 0) // SSM_GROUP) % per
    lane_group = lax.broadcasted_iota(jnp.int32, (n, LANES), 1) // SSM_GROUP
    taps = jnp.where(row_group == lane_group, _dot(kc_ref[0], rep_c_ref[...]), 0.0).astype(BF16)
    zero = jnp.zeros((LANES, LANES), BF16)
    for s in range(SSM_CHUNK):
        for t in range(SSM_CHUNK):
            lag = t - s
            m_ref[s * LANES:(s + 1) * LANES, t * LANES:(t + 1) * LANES] = (
                taps[lag * LANES:(lag + 1) * LANES] if lag >= 0 else zero)
    wc = wc_ref[0].astype(F32)
    for j in range(per):
        win_ref[:, j * LANES:(j + 1) * LANES] = jnp.where(row_group == j, wc, 0.0).astype(BF16)
    oc = oc_ref[0]
    width = 2 * LANES
    state_group = lax.broadcasted_iota(jnp.int32, (oc.shape[0], width), 0) // (2 * SSM_STATE)
    out_group = (lax.broadcasted_iota(jnp.int32, (oc.shape[0], width), 1) // SSM_GROUP) % per
    for nb in range(n // width):
        cols = slice(nb * width, (nb + 1) * width)
        wout_ref[:, cols] = jnp.where(state_group == out_group, _dot(oc, rep_tc_ref[:, cols]), 0.0).astype(BF16)


def _s5_body(u_ref, kc_ref, wc_ref, oc_ref, rep_c_ref, rep_tc_ref, a1_ref, a2_ref, d_ref, y_ref,
             carry_ref, m_ref, win_ref, wout_ref, *, n_steps):
    @pl.when((pl.program_id(1) == 0) & (pl.program_id(2) == 0))
    def _():
        _s5_expand_operators(kc_ref, wc_ref, oc_ref, rep_c_ref, rep_tc_ref, m_ref, win_ref, wout_ref)

    @pl.when(pl.program_id(2) == 0)
    def _():
        carry_ref[...] = jnp.zeros_like(carry_ref)

    n_rows = u_ref.shape[1] // SSM_CHUNK
    slabs = [u_ref[0, pl.ds(s, n_rows, stride=SSM_CHUNK), :] for s in range(SSM_CHUNK)]
    ucat = jnp.concatenate([slab.astype(BF16) for slab in slabs], axis=1)
    x = _dot(ucat, win_ref[...])
    rows = lax.broadcasted_iota(jnp.int32, (n_rows, LANES), 0)
    x_prev = []
    for i in range(LANES // SSM_GROUP):
        lanes = slice(i * LANES, (i + 1) * LANES)
        a1 = a1_ref[0, :, lanes]
        a2 = a2_ref[0, :, lanes]
        x_in = carry_ref[:, lanes]

        def times_a(v, k):
            return a1[k:k + 1] * v + a2[k:k + 1] * pltpu.roll(v, SSM_STATE, axis=1)

        xi = x[:, lanes] + jnp.where(rows == 0, times_a(x_in, 0)[0:1], 0.0)
        for k in range(n_steps):
            xi = xi + times_a(jnp.where(rows >= (1 << k), pltpu.roll(xi, 1 << k, axis=0), 0.0), k)
        x_prev.append(jnp.where(rows == 0, x_in[0:1], pltpu.roll(xi, 1, axis=0)))
        carry_ref[:, lanes] = jnp.broadcast_to(xi[n_rows - 1:n_rows], (SUBLANES, LANES))
    x_prev = jnp.concatenate(x_prev, axis=1).astype(BF16)
    width = 2 * LANES
    for nb in range(SSM_CHUNK * LANES // width):
        cols = slice(nb * width, (nb + 1) * width)
        hi = (nb + 1) * width
        y = _dot(ucat[:, :hi], m_ref[:hi, cols]) + _dot(x_prev, wout_ref[:, cols])
        for t in range(2 * nb, 2 * nb + 2):
            yt = y[:, (t - 2 * nb) * LANES:(t - 2 * nb + 1) * LANES] + d_ref[0] * slabs[t]
            y_ref[0, pl.ds(t, n_rows, stride=SSM_CHUNK), :] = _gelu_tanh(yt)


def _s5(u, kc, wc, oc, a1, a2, d_t):
    bsz, seq, width = u.shape
    n_blocks = width // LANES
    tile = min(SSM_TILE, seq)
    n_steps = a1.shape[1]
    assert (1 << n_steps) == tile // SSM_CHUNK and seq % tile == 0
    n_in = SSM_CHUNK * LANES
    n_state = (LANES // SSM_GROUP) * 2 * SSM_STATE
    chan = np.arange(LANES) % SSM_GROUP
    rep_c = jnp.asarray(np.arange(SSM_GROUP)[:, None] == chan[None, :], BF16)
    tc_in = np.arange(SSM_CHUNK * SSM_GROUP)
    tc_out = (np.arange(n_in) // LANES) * SSM_GROUP + np.arange(n_in) % SSM_GROUP
    rep_tc = jnp.asarray(tc_in[:, None] == tc_out[None, :], BF16)
    body = functools.partial(_s5_body, n_steps=n_steps)
    per_block = lambda a: pl.BlockSpec((1,) + a.shape[1:], lambda v, b, i: (v, 0, 0))
    const = lambda a: pl.BlockSpec(a.shape, lambda v, b, i: (0, 0))
    tokens = pl.BlockSpec((1, tile, LANES), lambda v, b, i: (b, i, v))
    return pl.pallas_call(
        body,
        name="s5",
        grid=(n_blocks, bsz, seq // tile),
        in_specs=[tokens, per_block(kc), per_block(wc), per_block(oc), const(rep_c), const(rep_tc),
                  per_block(a1), per_block(a2), per_block(d_t)],
        out_specs=tokens,
        out_shape=jax.ShapeDtypeStruct(u.shape, F32),
        scratch_shapes=[pltpu.VMEM((SUBLANES, n_state), F32), pltpu.VMEM((n_in, n_in), BF16),
                        pltpu.VMEM((n_in, n_state), BF16), pltpu.VMEM((n_state, n_in), BF16)],
        compiler_params=pltpu.CompilerParams(
            dimension_semantics=("arbitrary", "arbitrary", "arbitrary"), vmem_limit_bytes=VMEM_LIMIT),
    )(u, kc, wc, oc, rep_c, rep_tc, a1, a2, d_t)


def _ssm_operators(a_re, a_im, log_dt, b_re, b_im, c_re, c_im, d_skip, n_steps):
    t_len = SSM_CHUNK
    groups, n_state = a_re.shape
    per = LANES // SSM_GROUP
    nv = groups // per
    hi = lax.Precision.HIGHEST
    dt = jnp.exp(log_dt)[:, None]
    mag = jnp.exp(a_re * dt)
    lb_re = mag * jnp.cos(a_im * dt)
    lb_im = mag * jnp.sin(a_im * dt)
    nr, ni = lb_re - 1.0, lb_im
    den = a_re * a_re + a_im * a_im
    cr = (nr * a_re + ni * a_im) / den
    ci = (ni * a_re - nr * a_im) / den
    bb_re = cr[..., None] * b_re - ci[..., None] * b_im
    bb_im = cr[..., None] * b_im + ci[..., None] * b_re

    def lam_pow(n):
        n = n.astype(F32)[:, None, None]
        m = jnp.exp(a_re * dt * n)
        return m * jnp.cos(a_im * dt * n), m * jnp.sin(a_im * dt * n)

    pw_re, pw_im = lam_pow(jnp.arange(t_len + 1))
    lbr = pw_re[..., None] * bb_re[None] - pw_im[..., None] * bb_im[None]
    lbi = pw_re[..., None] * bb_im[None] + pw_im[..., None] * bb_re[None]
    taps = (jnp.einsum('gcp,tgpd->tgcd', c_re, lbr[:t_len], precision=hi)
            - jnp.einsum('gcp,tgpd->tgcd', c_im, lbi[:t_len], precision=hi))
    kc = taps.reshape(t_len, nv, per, SSM_GROUP, SSM_GROUP).transpose(1, 0, 2, 4, 3)
    kc = kc.reshape(nv, t_len * LANES, SSM_GROUP).astype(BF16)
    lb = jnp.stack([lbr[:t_len][::-1], lbi[:t_len][::-1]], axis=0)
    lb = lb.reshape(2, t_len, nv, per, n_state, SSM_GROUP).transpose(2, 1, 3, 5, 0, 4)
    wc = lb.reshape(nv, t_len * LANES, 2 * n_state).astype(BF16)
    p_re = pw_re[1:, :, None, :]
    p_im = pw_im[1:, :, None, :]
    clr = c_re[None] * p_re - c_im[None] * p_im
    cli = c_re[None] * p_im + c_im[None] * p_re
    cl = jnp.stack([clr, -cli], axis=0)
    cl = cl.reshape(2, t_len, nv, per, SSM_GROUP, n_state).transpose(2, 3, 0, 5, 1, 4)
    oc = cl.reshape(nv, per * 2 * n_state, t_len * SSM_GROUP).astype(BF16)
    pa_re, pa_im = lam_pow(t_len * (2 ** jnp.arange(n_steps)))
    by_block = lambda a: a.reshape(n_steps, nv, per * 2 * n_state).transpose(1, 0, 2)
    a1 = by_block(jnp.concatenate([pa_re, pa_re], axis=-1))
    a2 = by_block(jnp.concatenate([-pa_im, pa_im], axis=-1))
    d_t = d_skip.reshape(nv, 1, LANES)
    return kc, wc, oc, a1, a2, d_t


def _combine_body(o_ref, gy_ref, sg_ref, x_ref, wap_ref, wglu_ref, bglu_ref, wsp_ref, wout_ref, g2_ref,
                  h_ref, hn_ref):
    attn = _dot(o_ref[0], wap_ref[...])
    g = gy_ref[0]
    s = g * jax.nn.sigmoid(_dot(g.astype(BF16), wglu_ref[...]) + bglu_ref[...])
    ssm = _dot(s.astype(BF16), wsp_ref[...])
    sg = sg_ref[0]
    d_model = attn.shape[1]
    mix = sg[:, :d_model] * attn + sg[:, d_model:] * ssm
    h = x_ref[0] + _dot(mix.astype(BF16), wout_ref[...])
    h_ref[0] = h
    hn_ref[0] = (h * _rms_scale(h) * g2_ref[...]).astype(BF16)


def _combine(o, gy, sg, x, wap, wglu, bglu, wsp, wout, g2):
    bsz, seq, d_model = x.shape
    tm = TOKEN_TILE
    d_u = gy.shape[-1]
    tile = lambda width: pl.BlockSpec((1, tm, width), lambda b, i: (b, i, 0))
    return pl.pallas_call(
        _combine_body,
        name="combine",
        grid=(bsz, seq // tm),
        in_specs=[tile(o.shape[-1]), tile(d_u), tile(2 * d_model), tile(d_model),
                  _resident(wap.shape), _resident(wglu.shape), _resident(bglu.shape),
                  _resident(wsp.shape), _resident(wout.shape), _resident(g2.shape)],
        out_specs=[tile(d_model), tile(d_model)],
        out_shape=[jax.ShapeDtypeStruct(x.shape, F32), jax.ShapeDtypeStruct(x.shape, BF16)],
        compiler_params=pltpu.CompilerParams(
            dimension_semantics=("arbitrary", "arbitrary"), vmem_limit_bytes=VMEM_LIMIT),
    )(o, gy, sg, x, wap, wglu, bglu, wsp, wout, g2)


def _ffn_body(hn_ref, h_ref, wup_ref, cw_ref, cb_ref, wdn_ref, y_ref, carry_ref, act_ref, upg_ref, upv_ref,
              *, d_ff):
    @pl.when(pl.program_id(1) == 0)
    def _():
        carry_ref[...] = jnp.zeros_like(carry_ref)

    hn = hn_ref[0]
    tm = hn.shape[0]

    def conv_cols(col0, up_ref):
        cols = pl.ds(col0, FFN_CHUNK)
        up = _dot(hn, wup_ref[:, cols])
        up_ref[0:SUBLANES, :] = carry_ref[:, cols]
        up_ref[SUBLANES:, :] = up
        carry_ref[:, cols] = up[tm - SUBLANES:, :]
        out = cb_ref[:, cols] + up * cw_ref[CONV_WIDTH - 1:CONV_WIDTH, cols]
        for tap in range(CONV_WIDTH - 1):
            lo = SUBLANES - (CONV_WIDTH - 1) + tap
            out = out + up_ref[lo:lo + tm, :] * cw_ref[tap:tap + 1, cols]
        return out

    for j in range(d_ff // FFN_CHUNK):
        gate = conv_cols(j * FFN_CHUNK, upg_ref)
        val = conv_cols(d_ff + j * FFN_CHUNK, upv_ref)
        act_ref[:, j * FFN_CHUNK:(j + 1) * FFN_CHUNK] = (gate * jax.nn.sigmoid(gate) * val).astype(BF16)
    y_ref[0] = h_ref[0] + _dot(act_ref[...], wdn_ref[...])


def _convffn(hn, h, wup, cw, cb, wdn):
    bsz, seq, d_model = h.shape
    tm = FFN_TILE
    d_ff = wdn.shape[0]
    assert d_ff % FFN_CHUNK == 0 and wup.shape[1] == 2 * d_ff
    tile = pl.BlockSpec((1, tm, d_model), lambda b, i: (b, i, 0))
    body = functools.partial(_ffn_body, d_ff=d_ff)
    return pl.pallas_call(
        body,
        name="convffn",
        grid=(bsz, seq // tm),
        in_specs=[tile, tile, _resident(wup.shape), _resident(cw.shape), _resident(cb.shape),
                  _resident(wdn.shape)],
        out_specs=tile,
        out_shape=jax.ShapeDtypeStruct(h.shape, F32),
        scratch_shapes=[pltpu.VMEM((SUBLANES, 2 * d_ff), F32), pltpu.VMEM((tm, d_ff), BF16),
                        pltpu.VMEM((SUBLANES + tm, FFN_CHUNK), F32), pltpu.VMEM((SUBLANES + tm, FFN_CHUNK), F32)],
        compiler_params=pltpu.CompilerParams(
            dimension_semantics=("arbitrary", "arbitrary"), vmem_limit_bytes=VMEM_LIMIT),
    )(hn, h, wup, cw, cb, wdn)


def kernel(x, norm1_gain, w_in, q_norm_gain, k_norm_gain, lambda_q1, lambda_k1, lambda_q2, lambda_k2,
           subln_gain, w_attn_proj, ssm_a_re, ssm_a_im, ssm_log_dt, ssm_b_re, ssm_b_im, ssm_c_re, ssm_c_im,
           ssm_d, w_glu, b_glu, w_ssm_proj, w_out, norm2_gain, w_up, conv_w, conv_b, w_down):
    bsz, seq, d_model = x.shape
    assert norm1_gain.shape[0] == 1, "single-layer trunk"
    assert seq % (TOKEN_TILE) == 0 and seq % SSM_CHUNK == 0
    d_qk = N_HEADS * HEAD_COLS

    w_in_b = _to_bf16(w_in[0], qk_cols=2 * d_qk)
    d_of_lane = _head_lane_dim()
    lane_gain = lambda g: jnp.tile(g[d_of_lane], N_HEADS)[None, :]
    gq = lane_gain(q_norm_gain[0].astype(F32)) * (LOG2E / math.sqrt(HEAD_DIM))
    gk = lane_gain(k_norm_gain[0].astype(F32))

    pos = np.arange(seq, dtype=np.float32)
    inv_freq = (1.0 / (np.float32(ROPE_THETA) ** (np.arange(0, HEAD_DIM, 2, dtype=np.float32) / np.float32(HEAD_DIM))))
    ang = pos[:, None] * inv_freq.astype(np.float32)[None, :]
    cos_t = jnp.asarray(np.tile(np.cos(ang), (1, HEAD_COLS // ROPE_HALF)), F32)
    sin_h = np.sin(ang)
    sin_t = jnp.asarray(np.concatenate([-sin_h, -sin_h, sin_h, sin_h], axis=1), F32)

    comp = np.tile(_head_lane_component(), N_HEADS)
    head = np.repeat(np.arange(N_HEADS), HEAD_COLS)
    e_np = np.zeros((d_qk, LANES), np.float32)
    e_np[np.arange(d_qk), head * 2 + comp] = 1.0
    e_mat = jnp.asarray(e_np, BF16)
    et_mat = jnp.asarray(np.concatenate([e_np.T, e_np.T], axis=0), BF16)

    qT, k, vT, u, sg = _inproj(x, norm1_gain.astype(F32), w_in_b, gq, gk, cos_t, sin_t, e_mat, et_mat)

    lam = (jnp.exp(jnp.sum(lambda_q1[0].astype(F32) * lambda_k1[0].astype(F32)))
           - jnp.exp(jnp.sum(lambda_q2[0].astype(F32) * lambda_k2[0].astype(F32))) + LAM_INIT)
    score_bound = HEAD_DIM * jnp.max(jnp.abs(gq)) * jnp.max(jnp.abs(gk))
    o = lax.cond(score_bound <= SCORE_LOG2_LIMIT,
                 functools.partial(_attention, bounded=True),
                 functools.partial(_attention, bounded=False),
                 lam.reshape(1, 1), qT, k, vT, subln_gain[0].astype(F32)[:, None])

    n_steps = int(math.log2(min(SSM_TILE, seq) // SSM_CHUNK))
    ops = _ssm_operators(ssm_a_re[0].astype(F32), ssm_a_im[0].astype(F32), ssm_log_dt[0].astype(F32),
                         ssm_b_re[0].astype(F32), ssm_b_im[0].astype(F32), ssm_c_re[0].astype(F32),
                         ssm_c_im[0].astype(F32), ssm_d[0].astype(F32), n_steps)
    gy = _s5(u, *ops)

    h, hn = _combine(o, gy, sg, x, _to_bf16(w_attn_proj[0]), _to_bf16(w_glu[0]),
                     b_glu.astype(F32), _to_bf16(w_ssm_proj[0]), _to_bf16(w_out[0]),
                     norm2_gain.astype(F32))
    return _convffn(hn, h, _to_bf16(w_up[0]), conv_w[0].astype(F32), conv_b.astype(F32),
                    _to_bf16(w_down[0]))
```

```python
import functools
import math

import numpy as np
import jax
import jax.numpy as jnp
from jax import lax
from jax.experimental import pallas as pl
from jax.experimental.pallas import tpu as pltpu

F32 = jnp.float32
BF16 = jnp.bfloat16

N_HEADS = 8
HEAD_DIM = 64
HEAD_COLS = 2 * HEAD_DIM
ROPE_HALF = HEAD_DIM // 2
V_DIM = 128
ROPE_THETA = 10000.0
SSM_GROUP = 16
SSM_GROUPS = 32
SSM_STATE = 64
SSM_CHUNK = 16
CONV_WIDTH = 3
EPS = 1e-6
LAM_INIT = 0.8 - 0.6 * math.exp(-0.3 * 0)

LANES = 128
SUBLANES = 8
TOKEN_TILE = 512
FFN_CHUNK = 256
FFN_TILE = 512
CAST_ROWS = 256
ATTN_UNROLL = 4
SSM_TILE = 4096
VMEM_LIMIT = 56 * 1024 * 1024
MASK_VALUE = -1e30
SCORE_LOG2_LIMIT = 40.0
LOG2E = math.log2(math.e)


def _dot(a, b):
    return jnp.dot(a, b, preferred_element_type=F32)


def _rms_scale(x):
    return lax.rsqrt(jnp.mean(x * x, axis=-1, keepdims=True) + EPS)


def _resident(shape):
    return pl.BlockSpec(shape, lambda *_: (0,) * len(shape), pipeline_mode=pl.Buffered(1))


def _to_bf16_body(w_ref, o_ref, *, qk_cols):
    w = w_ref[...]
    if qk_cols:
        head = w[:, :qk_cols]
        quarter = (lax.broadcasted_iota(jnp.int32, head.shape, 1) % HEAD_COLS) // ROPE_HALF
        from_right = pltpu.roll(head, qk_cols - ROPE_HALF, axis=1)
        from_left = pltpu.roll(head, ROPE_HALF, axis=1)
        head = jnp.where(quarter == 1, from_right, jnp.where(quarter == 2, from_left, head))
        o_ref[:, :qk_cols] = head.astype(BF16)
        o_ref[:, qk_cols:] = w[:, qk_cols:].astype(BF16)
    else:
        o_ref[...] = w.astype(BF16)


def _to_bf16(w, qk_cols=0):
    rows, cols = w.shape
    blk = min(rows, CAST_ROWS)
    assert rows % blk == 0
    return pl.pallas_call(
        functools.partial(_to_bf16_body, qk_cols=qk_cols),
        name="to_bf16",
        grid=(rows // blk,),
        in_specs=[pl.BlockSpec((blk, cols), lambda i: (i, 0))],
        out_specs=pl.BlockSpec((blk, cols), lambda i: (i, 0)),
        out_shape=jax.ShapeDtypeStruct(w.shape, BF16),
        compiler_params=pltpu.CompilerParams(dimension_semantics=("arbitrary",), vmem_limit_bytes=VMEM_LIMIT),
    )(w)


def _head_lane_component():
    j = np.arange(HEAD_COLS)
    return (j // ROPE_HALF) % 2


def _head_lane_dim():
    j = np.arange(HEAD_COLS)
    return (j // HEAD_DIM) * ROPE_HALF + j % ROPE_HALF


def _inproj_body(x_ref, g1_ref, w_ref, gq_ref, gk_ref, cos_ref, sin_ref, e_ref, et_ref,
                 qT_ref, k_ref, vT_ref, u_ref, sg_ref, *, d_qk, d_v, d_u):
    x = x_ref[0]
    xn = (x * _rms_scale(x) * g1_ref[...]).astype(BF16)
    cos = cos_ref[...]
    sin = sin_ref[...]

    def norm_rope(col0, gain_ref):
        t = _dot(xn, w_ref[:, col0:col0 + d_qk])
        ss = _dot((t * t).astype(BF16), e_ref[...])
        r = lax.rsqrt(ss * (1.0 / HEAD_DIM) + EPS)
        r_hi = r.astype(BF16)
        r_lo = (r - r_hi.astype(F32)).astype(BF16)
        rf = _dot(jnp.concatenate([r_hi, r_lo], axis=1), et_ref[...])
        t = t * rf * gain_ref[...]
        heads = []
        for h in range(N_HEADS):
            th = t[:, h * HEAD_COLS:(h + 1) * HEAD_COLS]
            heads.append(th * cos + pltpu.roll(th, HEAD_DIM, axis=1) * sin)
        return jnp.concatenate(heads, axis=1)

    q = norm_rope(0, gq_ref)
    qT_ref[0, 0] = q.T.astype(BF16)
    k = norm_rope(d_qk, gk_ref)
    k_ref[0] = k.astype(BF16)
    c0 = 2 * d_qk
    v = _dot(xn, w_ref[:, c0:c0 + d_v])
    vT_ref[0, 0] = v.T.astype(BF16)
    c0 += d_v
    u_ref[0] = _dot(xn, w_ref[:, c0:c0 + d_u])
    c0 += d_u
    sg_ref[0] = jax.nn.sigmoid(_dot(xn, w_ref[:, c0:]))


def _inproj(x, g1, w_in_b, gq, gk, cos_t, sin_t, e_mat, et_mat):
    bsz, seq, d_model = x.shape
    tm = TOKEN_TILE
    nt = seq // tm
    d_qk = N_HEADS * HEAD_COLS
    d_v = N_HEADS * V_DIM
    d_u = SSM_GROUPS * SSM_GROUP
    d_g = 2 * d_model
    in_cols = w_in_b.shape[1]
    assert in_cols == 2 * d_qk + d_v + d_u + d_g and seq % tm == 0
    body = functools.partial(_inproj_body, d_qk=d_qk, d_v=d_v, d_u=d_u)
    return pl.pallas_call(
        body,
        name="inproj",
        grid=(bsz, nt),
        in_specs=[
            pl.BlockSpec((1, tm, d_model), lambda b, i: (b, i, 0)),
            _resident((1, d_model)),
            _resident((d_model, in_cols)),
            _resident((1, d_qk)),
            _resident((1, d_qk)),
            pl.BlockSpec((tm, LANES), lambda b, i: (i, 0)),
            pl.BlockSpec((tm, LANES), lambda b, i: (i, 0)),
            _resident((d_qk, LANES)),
            _resident((2 * LANES, d_qk)),
        ],
        out_specs=[
            pl.BlockSpec((1, 1, d_qk, tm), lambda b, i: (b, i, 0, 0)),
            pl.BlockSpec((1, tm, d_qk), lambda b, i: (b, i, 0)),
            pl.BlockSpec((1, 1, d_v, tm), lambda b, i: (b, i, 0, 0)),
            pl.BlockSpec((1, tm, d_u), lambda b, i: (b, i, 0)),
            pl.BlockSpec((1, tm, d_g), lambda b, i: (b, i, 0)),
        ],
        out_shape=[
            jax.ShapeDtypeStruct((bsz, nt, d_qk, tm), BF16),
            jax.ShapeDtypeStruct((bsz, seq, d_qk), BF16),
            jax.ShapeDtypeStruct((bsz, nt, d_v, tm), BF16),
            jax.ShapeDtypeStruct((bsz, seq, d_u), F32),
            jax.ShapeDtypeStruct((bsz, seq, d_g), F32),
        ],
        compiler_params=pltpu.CompilerParams(
            dimension_semantics=("arbitrary", "arbitrary"), vmem_limit_bytes=VMEM_LIMIT),
    )(x, g1, w_in_b, gq, gk, cos_t, sin_t, e_mat, et_mat)


def _split_components(qT):
    row = lax.broadcasted_iota(jnp.int32, qT.shape, 0)
    is_c0 = ((row // ROPE_HALF) % 2) == 0
    zero = jnp.zeros_like(qT)
    return jnp.where(is_c0, qT, zero), jnp.where(is_c0, zero, qT)


def _causal(shape):
    return lax.broadcasted_iota(jnp.int32, shape, 0) <= lax.broadcasted_iota(jnp.int32, shape, 1)


def _attn_finish(lam_ref, gsub_ref, acc0, acc1, l0, l1):
    lam = lam_ref[0, 0]
    a = acc0 * (1.0 / l0) - lam * (acc1 * (1.0 / l1))
    a = a * lax.rsqrt(jnp.mean(a * a, axis=0, keepdims=True) + EPS)
    a = a * gsub_ref[...] * (1.0 - LAM_INIT)
    return a.T.astype(BF16)


def _attn_online_body(lam_ref, qT_ref, k_ref, vT_ref, gsub_ref, o_ref, acc0_ref, acc1_ref, *, blk):
    qi = pl.program_id(2)
    q_comp = _split_components(qT_ref[0, 0])
    accs = (acc0_ref, acc1_ref)
    acc0_ref[...] = jnp.zeros_like(acc0_ref)
    acc1_ref[...] = jnp.zeros_like(acc1_ref)

    def block(j, carry, masked):
        kj = k_ref[0, pl.ds(pl.multiple_of(j * blk, blk), blk), :]
        vj = vT_ref[0, j]
        out = []
        for c in range(2):
            m, l = carry[c]
            sT = _dot(kj, q_comp[c])
            if masked:
                sT = jnp.where(_causal(sT.shape), sT, MASK_VALUE)
            m_new = jnp.maximum(m, jnp.max(sT, axis=0, keepdims=True))
            alpha = jnp.exp2(m - m_new)
            p = jnp.exp2(sT - m_new)
            out.append((m_new, alpha * l + jnp.sum(p, axis=0, keepdims=True)))
            accs[c][...] = accs[c][...] * alpha + _dot(vj, p.astype(BF16))
        return tuple(out)

    init = ((jnp.full((1, blk), MASK_VALUE, F32), jnp.zeros((1, blk), F32)),) * 2
    carry = lax.fori_loop(0, qi, lambda j, c: block(j, c, False), init)
    (_, l0), (_, l1) = block(qi, carry, True)
    o_ref[0] = _attn_finish(lam_ref, gsub_ref, acc0_ref[...], acc1_ref[...], l0, l1)


def _attn_bounded_body(lam_ref, qT_ref, k_ref, vT_ref, gsub_ref, o_ref, acc_ref, sa_ref, sb_ref, *, blk):
    qi = pl.program_id(2)
    heads = tuple(slice(s * HEAD_COLS, (s + 1) * HEAD_COLS) for s in range(2))
    q_comp = [_split_components(qT_ref[0, 0, hs, :]) for hs in heads]
    acc_ref[...] = jnp.zeros_like(acc_ref)
    bufs = (sa_ref, sb_ref)

    half = blk // 2

    def scores(s, j, diagonal=False):
        kj = k_ref[0, pl.ds(pl.multiple_of(j * blk, blk), blk), heads[s]]
        for c in range(2):
            if diagonal:
                bufs[s][c, 0:half, :] = _dot(kj[0:half], q_comp[s][c])
                bufs[s][c, half:, half:] = _dot(kj[half:], q_comp[s][c][:, half:])
            else:
                bufs[s][c] = _dot(kj, q_comp[s][c])

    def consume(s, j, carry, masked):
        vj = vT_ref[0, j, heads[s], :]
        parts = list(carry)
        sum8 = lambda p: jnp.sum(p.reshape(p.shape[0] // SUBLANES, SUBLANES, p.shape[1]), axis=0)
        for c in range(2):
            if masked:
                p_lo = jnp.exp2(bufs[s][c, 0:half, :])
                p_lo = jnp.where(_causal(p_lo.shape), p_lo, 0.0)
                p_hi = jnp.exp2(bufs[s][c, half:, half:])
                p_hi = jnp.where(_causal(p_hi.shape), p_hi, 0.0)
                parts[2 * s + c] = (parts[2 * s + c] + sum8(p_lo)
                                    + jnp.concatenate([jnp.zeros((SUBLANES, half), F32), sum8(p_hi)], axis=1))
                acc_ref[s, c] += _dot(vj[:, 0:half], p_lo.astype(BF16))
                acc_ref[s, c, :, half:] += _dot(vj[:, half:], p_hi.astype(BF16))
            else:
                p_lo = jnp.exp2(bufs[s][c, 0:half, :])
                p_hi = jnp.exp2(bufs[s][c, half:, :])
                parts[2 * s + c] = parts[2 * s + c] + sum8(p_lo) + sum8(p_hi)
                acc_ref[s, c] += _dot(vj[:, 0:half], p_lo.astype(BF16)) + _dot(vj[:, half:], p_hi.astype(BF16))
        return tuple(parts)

    def run(first, count, carry, diagonal_last):
        n_units = 2 * count
        for u in range(n_units):
            if u + 1 < n_units or not diagonal_last:
                scores((u + 1) % 2, first + (u + 1) // 2, diagonal_last and (u + 1) // 2 == count - 1)
            carry = consume(u % 2, first + u // 2, carry, diagonal_last and u // 2 == count - 1)
        return carry

    scores(0, 0)
    trips = qi // ATTN_UNROLL
    carry = lax.fori_loop(0, trips, lambda t, c: run(ATTN_UNROLL * t, ATTN_UNROLL, c, False),
                          (jnp.zeros((SUBLANES, blk), F32),) * 4)
    first = trips * ATTN_UNROLL
    tails = [functools.partial(run, first, rem + 1, diagonal_last=True) for rem in range(ATTN_UNROLL)]
    carry = lax.switch(qi - first, tails, carry)
    l = [jnp.sum(part, axis=0, keepdims=True) for part in carry]
    for s in range(2):
        o_ref[0, :, heads[s]] = _attn_finish(lam_ref, gsub_ref, acc_ref[s, 0], acc_ref[s, 1], l[2 * s], l[2 * s + 1])


def _attention(lam, qT, k, vT, gsub, *, bounded):
    bsz, nblk, d_qk, blk = qT.shape
    seq = k.shape[1]
    if bounded:
        group = 2
        body = functools.partial(_attn_bounded_body, blk=blk)
        scratch = [pltpu.VMEM((group, 2, V_DIM, blk), F32),
                   pltpu.VMEM((2, blk, blk), F32), pltpu.VMEM((2, blk, blk), F32)]
    else:
        group = 1
        body = functools.partial(_attn_online_body, blk=blk)
        scratch = [pltpu.VMEM((V_DIM, blk), F32), pltpu.VMEM((V_DIM, blk), F32)]
    return pl.pallas_call(
        body,
        name="attention_bounded" if bounded else "attention_online",
        grid=(bsz, N_HEADS // group, nblk),
        in_specs=[
            pl.BlockSpec(memory_space=pltpu.SMEM),
            pl.BlockSpec((1, 1, group * HEAD_COLS, blk), lambda b, h, i: (b, i, h, 0)),
            pl.BlockSpec((1, seq, group * HEAD_COLS), lambda b, h, i: (b, 0, h)),
            pl.BlockSpec((1, nblk, group * V_DIM, blk), lambda b, h, i: (b, 0, h, 0)),
            pl.BlockSpec((V_DIM, 1), lambda b, h, i: (0, 0)),
        ],
        out_specs=pl.BlockSpec((1, blk, group * V_DIM), lambda b, h, i: (b, i, h)),
        out_shape=jax.ShapeDtypeStruct((bsz, seq, N_HEADS * V_DIM), BF16),
        scratch_shapes=scratch,
        compiler_params=pltpu.CompilerParams(
            dimension_semantics=("arbitrary", "arbitrary", "arbitrary"), vmem_limit_bytes=VMEM_LIMIT),
    )(lam, qT, k, vT, gsub)


def _gelu_tanh(y):
    return 0.5 * y * (1.0 + jnp.tanh(math.sqrt(2.0 / math.pi) * (y + 0.044715 * (y * y * y))))


def _s5_expand_operators(kc_ref, wc_ref, oc_ref, rep_c_ref, rep_tc_ref, m_ref, win_ref, wout_ref):
    per = LANES // SSM_GROUP
    n = SSM_CHUNK * LANES
    row_group = (lax.broadcasted_iota(jnp.int32, (n, LANES), 0) // SSM_GROUP) % per
    lane_group = lax.broadcasted_iota(jnp.int32, (n, LANES), 1) // SSM_GROUP
    taps = jnp.where(row_group == lane_group, _dot(kc_ref[0], rep_c_ref[...]), 0.0).astype(BF16)
    zero = jnp.zeros((LANES, LANES), BF16)
    for s in range(SSM_CHUNK):
        for t in range(SSM_CHUNK):
            lag = t - s
            m_ref[s * LANES:(s + 1) * LANES, t * LANES:(t + 1) * LANES] = (
                taps[lag * LANES:(lag + 1) * LANES] if lag >= 0 else zero)
    wc = wc_ref[0].astype(F32)
    for j in range(per):
        win_ref[:, j * LANES:(j + 1) * LANES] = jnp.where(row_group == j, wc, 0.0).astype(BF16)
    oc = oc_ref[0]
    width = 2 * LANES
    state_group = lax.broadcasted_iota(jnp.int32, (oc.shape[0], width), 0) // (2 * SSM_STATE)
    out_group = (lax.broadcasted_iota(jnp.int32, (oc.shape[0], width), 1) // SSM_GROUP) % per
    for nb in range(n // width):
        cols = slice(nb * width, (nb + 1) * width)
        wout_ref[:, cols] = jnp.where(state_group == out_group, _dot(oc, rep_tc_ref[:, cols]), 0.0).astype(BF16)


def _s5_body(u_ref, kc_ref, wc_ref, oc_ref, rep_c_ref, rep_tc_ref, a1_ref, a2_ref, d_ref, y_ref,
             carry_ref, m_ref, win_ref, wout_ref, *, n_steps):
    @pl.when((pl.program_id(1) == 0) & (pl.program_id(2) == 0))
    def _():
        _s5_expand_operators(kc_ref, wc_ref, oc_ref, rep_c_ref, rep_tc_ref, m_ref, win_ref, wout_ref)

    @pl.when(pl.program_id(2) == 0)
    def _():
        carry_ref[...] = jnp.zeros_like(carry_ref)

    n_rows = u_ref.shape[1] // SSM_CHUNK
    slabs = [u_ref[0, pl.ds(s, n_rows, stride=SSM_CHUNK), :] for s in range(SSM_CHUNK)]
    ucat = jnp.concatenate([slab.astype(BF16) for slab in slabs], axis=1)
    x = _dot(ucat, win_ref[...])
    rows = lax.broadcasted_iota(jnp.int32, (n_rows, LANES), 0)
    x_prev = []
    for i in range(LANES // SSM_GROUP):
        lanes = slice(i * LANES, (i + 1) * LANES)
        a1 = a1_ref[0, :, lanes]
        a2 = a2_ref[0, :, lanes]
        x_in = carry_ref[:, lanes]

        def times_a(v, k):
            return a1[k:k + 1] * v + a2[k:k + 1] * pltpu.roll(v, SSM_STATE, axis=1)

        xi = x[:, lanes] + jnp.where(rows == 0, times_a(x_in, 0)[0:1], 0.0)
        for k in range(n_steps):
            xi = xi + times_a(jnp.where(rows >= (1 << k), pltpu.roll(xi, 1 << k, axis=0), 0.0), k)
        x_prev.append(jnp.where(rows == 0, x_in[0:1], pltpu.roll(xi, 1, axis=0)))
        carry_ref[:, lanes] = jnp.broadcast_to(xi[n_rows - 1:n_rows], (SUBLANES, LANES))
    x_prev = jnp.concatenate(x_prev, axis=1).astype(BF16)
    width = 2 * LANES
    for nb in range(SSM_CHUNK * LANES // width):
        cols = slice(nb * width, (nb + 1) * width)
        hi = (nb + 1) * width
        y = _dot(ucat[:, :hi], m_ref[:hi, cols]) + _dot(x_prev, wout_ref[:, cols])
        for t in range(2 * nb, 2 * nb + 2):
            yt = y[:, (t - 2 * nb) * LANES:(t - 2 * nb + 1) * LANES] + d_ref[0] * slabs[t]
            y_ref[0, pl.ds(t, n_rows, stride=SSM_CHUNK), :] = _gelu_tanh(yt)


def _s5(u, kc, wc, oc, a1, a2, d_t):
    bsz, seq, width = u.shape
    n_blocks = width // LANES
    tile = min(SSM_TILE, seq)
    n_steps = a1.shape[1]
    assert (1 << n_steps) == tile // SSM_CHUNK and seq % tile == 0
    n_in = SSM_CHUNK * LANES
    n_state = (LANES // SSM_GROUP) * 2 * SSM_STATE
    chan = np.arange(LANES) % SSM_GROUP
    rep_c = jnp.asarray(np.arange(SSM_GROUP)[:, None] == chan[None, :], BF16)
    tc_in = np.arange(SSM_CHUNK * SSM_GROUP)
    tc_out = (np.arange(n_in) // LANES) * SSM_GROUP + np.arange(n_in) % SSM_GROUP
    rep_tc = jnp.asarray(tc_in[:, None] == tc_out[None, :], BF16)
    body = functools.partial(_s5_body, n_steps=n_steps)
    per_block = lambda a: pl.BlockSpec((1,) + a.shape[1:], lambda v, b, i: (v, 0, 0))
    const = lambda a: pl.BlockSpec(a.shape, lambda v, b, i: (0, 0))
    tokens = pl.BlockSpec((1, tile, LANES), lambda v, b, i: (b, i, v))
    return pl.pallas_call(
        body,
        name="s5",
        grid=(n_blocks, bsz, seq // tile),
        in_specs=[tokens, per_block(kc), per_block(wc), per_block(oc), const(rep_c), const(rep_tc),
                  per_block(a1), per_block(a2), per_block(d_t)],
        out_specs=tokens,
        out_shape=jax.ShapeDtypeStruct(u.shape, F32),
        scratch_shapes=[pltpu.VMEM((SUBLANES, n_state), F32), pltpu.VMEM((n_in, n_in), BF16),
                        pltpu.VMEM((n_in, n_state), BF16), pltpu.VMEM((n_state, n_in), BF16)],
        compiler_params=pltpu.CompilerParams(
            dimension_semantics=("arbitrary", "arbitrary", "arbitrary"), vmem_limit_bytes=VMEM_LIMIT),
    )(u, kc, wc, oc, rep_c, rep_tc, a1, a2, d_t)


def _ssm_operators(a_re, a_im, log_dt, b_re, b_im, c_re, c_im, d_skip, n_steps):
    t_len = SSM_CHUNK
    groups, n_state = a_re.shape
    per = LANES // SSM_GROUP
    nv = groups // per
    hi = lax.Precision.HIGHEST
    dt = jnp.exp(log_dt)[:, None]
    mag = jnp.exp(a_re * dt)
    lb_re = mag * jnp.cos(a_im * dt)
    lb_im = mag * jnp.sin(a_im * dt)
    nr, ni = lb_re - 1.0, lb_im
    den = a_re * a_re + a_im * a_im
    cr = (nr * a_re + ni * a_im) / den
    ci = (ni * a_re - nr * a_im) / den
    bb_re = cr[..., None] * b_re - ci[..., None] * b_im
    bb_im = cr[..., None] * b_im + ci[..., None] * b_re

    def lam_pow(n):
        n = n.astype(F32)[:, None, None]
        m = jnp.exp(a_re * dt * n)
        return m * jnp.cos(a_im * dt * n), m * jnp.sin(a_im * dt * n)

    pw_re, pw_im = lam_pow(jnp.arange(t_len + 1))
    lbr = pw_re[..., None] * bb_re[None] - pw_im[..., None] * bb_im[None]
    lbi = pw_re[..., None] * bb_im[None] + pw_im[..., None] * bb_re[None]
    taps = (jnp.einsum('gcp,tgpd->tgcd', c_re, lbr[:t_len], precision=hi)
            - jnp.einsum('gcp,tgpd->tgcd', c_im, lbi[:t_len], precision=hi))
    kc = taps.reshape(t_len, nv, per, SSM_GROUP, SSM_GROUP).transpose(1, 0, 2, 4, 3)
    kc = kc.reshape(nv, t_len * LANES, SSM_GROUP).astype(BF16)
    lb = jnp.stack([lbr[:t_len][::-1], lbi[:t_len][::-1]], axis=0)
    lb = lb.reshape(2, t_len, nv, per, n_state, SSM_GROUP).transpose(2, 1, 3, 5, 0, 4)
    wc = lb.reshape(nv, t_len * LANES, 2 * n_state).astype(BF16)
    p_re = pw_re[1:, :, None, :]
    p_im = pw_im[1:, :, None, :]
    clr = c_re[None] * p_re - c_im[None] * p_im
    cli = c_re[None] * p_im + c_im[None] * p_re
    cl = jnp.stack([clr, -cli], axis=0)
    cl = cl.reshape(2, t_len, nv, per, SSM_GROUP, n_state).transpose(2, 3, 0, 5, 1, 4)
    oc = cl.reshape(nv, per * 2 * n_state, t_len * SSM_GROUP).astype(BF16)
    pa_re, pa_im = lam_pow(t_len * (2 ** jnp.arange(n_steps)))
    by_block = lambda a: a.reshape(n_steps, nv, per * 2 * n_state).transpose(1, 0, 2)
    a1 = by_block(jnp.concatenate([pa_re, pa_re], axis=-1))
    a2 = by_block(jnp.concatenate([-pa_im, pa_im], axis=-1))
    d_t = d_skip.reshape(nv, 1, LANES)
    return kc, wc, oc, a1, a2, d_t


def _combine_body(o_ref, gy_ref, sg_ref, x_ref, wap_ref, wglu_ref, bglu_ref, wsp_ref, wout_ref, g2_ref,
                  h_ref, hn_ref):
    attn = _dot(o_ref[0], wap_ref[...])
    g = gy_ref[0]
    s = g * jax.nn.sigmoid(_dot(g.astype(BF16), wglu_ref[...]) + bglu_ref[...])
    ssm = _dot(s.astype(BF16), wsp_ref[...])
    sg = sg_ref[0]
    d_model = attn.shape[1]
    mix = sg[:, :d_model] * attn + sg[:, d_model:] * ssm
    h = x_ref[0] + _dot(mix.astype(BF16), wout_ref[...])
    h_ref[0] = h
    hn_ref[0] = (h * _rms_scale(h) * g2_ref[...]).astype(BF16)


def _combine(o, gy, sg, x, wap, wglu, bglu, wsp, wout, g2):
    bsz, seq, d_model = x.shape
    tm = TOKEN_TILE
    d_u = gy.shape[-1]
    tile = lambda width: pl.BlockSpec((1, tm, width), lambda b, i: (b, i, 0))
    return pl.pallas_call(
        _combine_body,
        name="combine",
        grid=(bsz, seq // tm),
        in_specs=[tile(o.shape[-1]), tile(d_u), tile(2 * d_model), tile(d_model),
                  _resident(wap.shape), _resident(wglu.shape), _resident(bglu.shape),
                  _resident(wsp.shape), _resident(wout.shape), _resident(g2.shape)],
        out_specs=[tile(d_model), tile(d_model)],
        out_shape=[jax.ShapeDtypeStruct(x.shape, F32), jax.ShapeDtypeStruct(x.shape, BF16)],
        compiler_params=pltpu.CompilerParams(
            dimension_semantics=("arbitrary", "arbitrary"), vmem_limit_bytes=VMEM_LIMIT),
    )(o, gy, sg, x, wap, wglu, bglu, wsp, wout, g2)


def _ffn_body(hn_ref, h_ref, wup_ref, cw_ref, cb_ref, wdn_ref, y_ref, carry_ref, act_ref, upg_ref, upv_ref,
              *, d_ff):
    @pl.when(pl.program_id(1) == 0)
    def _():
        carry_ref[...] = jnp.zeros_like(carry_ref)

    hn = hn_ref[0]
    tm = hn.shape[0]

    def conv_cols(col0, up_ref):
        cols = pl.ds(col0, FFN_CHUNK)
        up = _dot(hn, wup_ref[:, cols])
        up_ref[0:SUBLANES, :] = carry_ref[:, cols]
        up_ref[SUBLANES:, :] = up
        carry_ref[:, cols] = up[tm - SUBLANES:, :]
        out = cb_ref[:, cols] + up * cw_ref[CONV_WIDTH - 1:CONV_WIDTH, cols]
        for tap in range(CONV_WIDTH - 1):
            lo = SUBLANES - (CONV_WIDTH - 1) + tap
            out = out + up_ref[lo:lo + tm, :] * cw_ref[tap:tap + 1, cols]
        return out

    for j in range(d_ff // FFN_CHUNK):
        gate = conv_cols(j * FFN_CHUNK, upg_ref)
        val = conv_cols(d_ff + j * FFN_CHUNK, upv_ref)
        act_ref[:, j * FFN_CHUNK:(j + 1) * FFN_CHUNK] = (gate * jax.nn.sigmoid(gate) * val).astype(BF16)
    y_ref[0] = h_ref[0] + _dot(act_ref[...], wdn_ref[...])


def _convffn(hn, h, wup, cw, cb, wdn):
    bsz, seq, d_model = h.shape
    tm = FFN_TILE
    d_ff = wdn.shape[0]
    assert d_ff % FFN_CHUNK == 0 and wup.shape[1] == 2 * d_ff
    tile = pl.BlockSpec((1, tm, d_model), lambda b, i: (b, i, 0))
    body = functools.partial(_ffn_body, d_ff=d_ff)
    return pl.pallas_call(
        body,
        name="convffn",
        grid=(bsz, seq // tm),
        in_specs=[tile, tile, _resident(wup.shape), _resident(cw.shape), _resident(cb.shape),
                  _resident(wdn.shape)],
        out_specs=tile,
        out_shape=jax.ShapeDtypeStruct(h.shape, F32),
        scratch_shapes=[pltpu.VMEM((SUBLANES, 2 * d_ff), F32), pltpu.VMEM((tm, d_ff), BF16),
                        pltpu.VMEM((SUBLANES + tm, FFN_CHUNK), F32), pltpu.VMEM((SUBLANES + tm, FFN_CHUNK), F32)],
        compiler_params=pltpu.CompilerParams(
            dimension_semantics=("arbitrary", "arbitrary"), vmem_limit_bytes=VMEM_LIMIT),
    )(hn, h, wup, cw, cb, wdn)


def kernel(x, norm1_gain, w_in, q_norm_gain, k_norm_gain, lambda_q1, lambda_k1, lambda_q2, lambda_k2,
           subln_gain, w_attn_proj, ssm_a_re, ssm_a_im, ssm_log_dt, ssm_b_re, ssm_b_im, ssm_c_re, ssm_c_im,
           ssm_d, w_glu, b_glu, w_ssm_proj, w_out, norm2_gain, w_up, conv_w, conv_b, w_down):
    bsz, seq, d_model = x.shape
    assert norm1_gain.shape[0] == 1, "single-layer trunk"
    assert seq % (TOKEN_TILE) == 0 and seq % SSM_CHUNK == 0
    d_qk = N_HEADS * HEAD_COLS

    w_in_b = _to_bf16(w_in[0], qk_cols=2 * d_qk)
    d_of_lane = _head_lane_dim()
    lane_gain = lambda g: jnp.tile(g[d_of_lane], N_HEADS)[None, :]
    gq = lane_gain(q_norm_gain[0].astype(F32)) * (LOG2E / math.sqrt(HEAD_DIM))
    gk = lane_gain(k_norm_gain[0].astype(F32))

    pos = np.arange(seq, dtype=np.float32)
    inv_freq = (1.0 / (np.float32(ROPE_THETA) ** (np.arange(0, HEAD_DIM, 2, dtype=np.float32) / np.float32(HEAD_DIM))))
    ang = pos[:, None] * inv_freq.astype(np.float32)[None, :]
    cos_t = jnp.asarray(np.tile(np.cos(ang), (1, HEAD_COLS // ROPE_HALF)), F32)
    sin_h = np.sin(ang)
    sin_t = jnp.asarray(np.concatenate([-sin_h, -sin_h, sin_h, sin_h], axis=1), F32)

    comp = np.tile(_head_lane_component(), N_HEADS)
    head = np.repeat(np.arange(N_HEADS), HEAD_COLS)
    e_np = np.zeros((d_qk, LANES), np.float32)
    e_np[np.arange(d_qk), head * 2 + comp] = 1.0
    e_mat = jnp.asarray(e_np, BF16)
    et_mat = jnp.asarray(np.concatenate([e_np.T, e_np.T], axis=0), BF16)

    qT, k, vT, u, sg = _inproj(x, norm1_gain.astype(F32), w_in_b, gq, gk, cos_t, sin_t, e_mat, et_mat)

    lam = (jnp.exp(jnp.sum(lambda_q1[0].astype(F32) * lambda_k1[0].astype(F32)))
           - jnp.exp(jnp.sum(lambda_q2[0].astype(F32) * lambda_k2[0].astype(F32))) + LAM_INIT)
    score_bound = HEAD_DIM * jnp.max(jnp.abs(gq)) * jnp.max(jnp.abs(gk))
    o = lax.cond(score_bound <= SCORE_LOG2_LIMIT,
                 functools.partial(_attention, bounded=True),
                 functools.partial(_attention, bounded=False),
                 lam.reshape(1, 1), qT, k, vT, subln_gain[0].astype(F32)[:, None])

    n_steps = int(math.log2(min(SSM_TILE, seq) // SSM_CHUNK))
    ops = _ssm_operators(ssm_a_re[0].astype(F32), ssm_a_im[0].astype(F32), ssm_log_dt[0].astype(F32),
                         ssm_b_re[0].astype(F32), ssm_b_im[0].astype(F32), ssm_c_re[0].astype(F32),
                         ssm_c_im[0].astype(F32), ssm_d[0].astype(F32), n_steps)
    gy = _s5(u, *ops)

    h, hn = _combine(o, gy, sg, x, _to_bf16(w_attn_proj[0]), _to_bf16(w_glu[0]),
                     b_glu.astype(F32), _to_bf16(w_ssm_proj[0]), _to_bf16(w_out[0]),
                     norm2_gain.astype(F32))
    return _convffn(hn, h, _to_bf16(w_up[0]), conv_w[0].astype(F32), conv_b.astype(F32),
                    _to_bf16(w_down[0]))
```

```python
import functools
import math

import numpy as np
import jax
import jax.numpy as jnp
from jax import lax
from jax.experimental import pallas as pl
from jax.experimental.pallas import tpu as pltpu

F32 = jnp.float32
BF16 = jnp.bfloat16

N_HEADS = 8
HEAD_DIM = 64
HEAD_COLS = 2 * HEAD_DIM
ROPE_HALF = HEAD_DIM // 2
V_DIM = 128
ROPE_THETA = 10000.0
SSM_GROUP = 16
SSM_GROUPS = 32
SSM_STATE = 64
SSM_CHUNK = 16
CONV_WIDTH = 3
EPS = 1e-6
LAM_INIT = 0.8 - 0.6 * math.exp(-0.3 * 0)

LANES = 128
SUBLANES = 8
TOKEN_TILE = 512
FFN_CHUNK = 256
FFN_TILE = 512
CAST_ROWS = 256
ATTN_UNROLL = 4
SSM_TILE = 4096
VMEM_LIMIT = 56 * 1024 * 1024
MASK_VALUE = -1e30
SCORE_LOG2_LIMIT = 40.0
LOG2E = math.log2(math.e)


def _dot(a, b):
    return jnp.dot(a, b, preferred_element_type=F32)


def _rms_scale(x):
    return lax.rsqrt(jnp.mean(x * x, axis=-1, keepdims=True) + EPS)


def _resident(shape):
    return pl.BlockSpec(shape, lambda *_: (0,) * len(shape), pipeline_mode=pl.Buffered(1))


def _to_bf16_body(w_ref, o_ref, *, qk_cols):
    w = w_ref[...]
    if qk_cols:
        head = w[:, :qk_cols]
        quarter = (lax.broadcasted_iota(jnp.int32, head.shape, 1) % HEAD_COLS) // ROPE_HALF
        from_right = pltpu.roll(head, qk_cols - ROPE_HALF, axis=1)
        from_left = pltpu.roll(head, ROPE_HALF, axis=1)
        head = jnp.where(quarter == 1, from_right, jnp.where(quarter == 2, from_left, head))
        o_ref[:, :qk_cols] = head.astype(BF16)
        o_ref[:, qk_cols:] = w[:, qk_cols:].astype(BF16)
    else:
        o_ref[...] = w.astype(BF16)


def _to_bf16(w, qk_cols=0):
    rows, cols = w.shape
    blk = min(rows, CAST_ROWS)
    assert rows % blk == 0
    return pl.pallas_call(
        functools.partial(_to_bf16_body, qk_cols=qk_cols),
        name="to_bf16",
        grid=(rows // blk,),
        in_specs=[pl.BlockSpec((blk, cols), lambda i: (i, 0))],
        out_specs=pl.BlockSpec((blk, cols), lambda i: (i, 0)),
        out_shape=jax.ShapeDtypeStruct(w.shape, BF16),
        compiler_params=pltpu.CompilerParams(dimension_semantics=("arbitrary",), vmem_limit_bytes=VMEM_LIMIT),
    )(w)


def _head_lane_component():
    j = np.arange(HEAD_COLS)
    return (j // ROPE_HALF) % 2


def _head_lane_dim():
    j = np.arange(HEAD_COLS)
    return (j // HEAD_DIM) * ROPE_HALF + j % ROPE_HALF


def _inproj_body(x_ref, g1_ref, w_ref, gq_ref, gk_ref, cos_ref, sin_ref, e_ref, et_ref,
                 qT_ref, k_ref, vT_ref, u_ref, sg_ref, *, d_qk, d_v, d_u):
    x = x_ref[0]
    xn = (x * _rms_scale(x) * g1_ref[...]).astype(BF16)
    cos = cos_ref[...]
    sin = sin_ref[...]

    def norm_rope(col0, gain_ref):
        t = _dot(xn, w_ref[:, col0:col0 + d_qk])
        ss = _dot((t * t).astype(BF16), e_ref[...])
        r = lax.rsqrt(ss * (1.0 / HEAD_DIM) + EPS)
        r_hi = r.astype(BF16)
        r_lo = (r - r_hi.astype(F32)).astype(BF16)
        rf = _dot(jnp.concatenate([r_hi, r_lo], axis=1), et_ref[...])
        t = t * rf * gain_ref[...]
        heads = []
        for h in range(N_HEADS):
            th = t[:, h * HEAD_COLS:(h + 1) * HEAD_COLS]
            heads.append(th * cos + pltpu.roll(th, HEAD_DIM, axis=1) * sin)
        return jnp.concatenate(heads, axis=1)

    q = norm_rope(0, gq_ref)
    qT_ref[0, 0] = q.T.astype(BF16)
    k = norm_rope(d_qk, gk_ref)
    k_ref[0] = k.astype(BF16)
    c0 = 2 * d_qk
    v = _dot(xn, w_ref[:, c0:c0 + d_v])
    vT_ref[0, 0] = v.T.astype(BF16)
    c0 += d_v
    u_ref[0] = _dot(xn, w_ref[:, c0:c0 + d_u])
    c0 += d_u
    sg_ref[0] = jax.nn.sigmoid(_dot(xn, w_ref[:, c0:])).astype(BF16)


def _inproj(x, g1, w_in_b, gq, gk, cos_t, sin_t, e_mat, et_mat):
    bsz, seq, d_model = x.shape
    tm = TOKEN_TILE
    nt = seq // tm
    d_qk = N_HEADS * HEAD_COLS
    d_v = N_HEADS * V_DIM
    d_u = SSM_GROUPS * SSM_GROUP
    d_g = 2 * d_model
    in_cols = w_in_b.shape[1]
    assert in_cols == 2 * d_qk + d_v + d_u + d_g and seq % tm == 0
    body = functools.partial(_inproj_body, d_qk=d_qk, d_v=d_v, d_u=d_u)
    return pl.pallas_call(
        body,
        name="inproj",
        grid=(bsz, nt),
        in_specs=[
            pl.BlockSpec((1, tm, d_model), lambda b, i: (b, i, 0)),
            _resident((1, d_model)),
            _resident((d_model, in_cols)),
            _resident((1, d_qk)),
            _resident((1, d_qk)),
            pl.BlockSpec((tm, LANES), lambda b, i: (i, 0)),
            pl.BlockSpec((tm, LANES), lambda b, i: (i, 0)),
            _resident((d_qk, LANES)),
            _resident((2 * LANES, d_qk)),
        ],
        out_specs=[
            pl.BlockSpec((1, 1, d_qk, tm), lambda b, i: (b, i, 0, 0)),
            pl.BlockSpec((1, tm, d_qk), lambda b, i: (b, i, 0)),
            pl.BlockSpec((1, 1, d_v, tm), lambda b, i: (b, i, 0, 0)),
            pl.BlockSpec((1, tm, d_u), lambda b, i: (b, i, 0)),
            pl.BlockSpec((1, tm, d_g), lambda b, i: (b, i, 0)),
        ],
        out_shape=[
            jax.ShapeDtypeStruct((bsz, nt, d_qk, tm), BF16),
            jax.ShapeDtypeStruct((bsz, seq, d_qk), BF16),
            jax.ShapeDtypeStruct((bsz, nt, d_v, tm), BF16),
            jax.ShapeDtypeStruct((bsz, seq, d_u), F32),
            jax.ShapeDtypeStruct((bsz, seq, d_g), BF16),
        ],
        compiler_params=pltpu.CompilerParams(
            dimension_semantics=("arbitrary", "arbitrary"), vmem_limit_bytes=VMEM_LIMIT),
    )(x, g1, w_in_b, gq, gk, cos_t, sin_t, e_mat, et_mat)


def _split_components(qT):
    row = lax.broadcasted_iota(jnp.int32, qT.shape, 0)
    is_c0 = ((row // ROPE_HALF) % 2) == 0
    zero = jnp.zeros_like(qT)
    return jnp.where(is_c0, qT, zero), jnp.where(is_c0, zero, qT)


def _causal(shape):
    return lax.broadcasted_iota(jnp.int32, shape, 0) <= lax.broadcasted_iota(jnp.int32, shape, 1)


def _attn_finish(lam_ref, gsub_ref, acc0, acc1, l0, l1):
    lam = lam_ref[0, 0]
    a = acc0 * (1.0 / l0) - lam * (acc1 * (1.0 / l1))
    a = a * lax.rsqrt(jnp.mean(a * a, axis=0, keepdims=True) + EPS)
    a = a * gsub_ref[...] * (1.0 - LAM_INIT)
    return a.T.astype(BF16)


def _attn_online_body(lam_ref, qT_ref, k_ref, vT_ref, gsub_ref, o_ref, acc0_ref, acc1_ref, *, blk):
    qi = pl.program_id(2)
    q_comp = _split_components(qT_ref[0, 0])
    accs = (acc0_ref, acc1_ref)
    acc0_ref[...] = jnp.zeros_like(acc0_ref)
    acc1_ref[...] = jnp.zeros_like(acc1_ref)

    def block(j, carry, masked):
        kj = k_ref[0, pl.ds(pl.multiple_of(j * blk, blk), blk), :]
        vj = vT_ref[0, j]
        out = []
        for c in range(2):
            m, l = carry[c]
            sT = _dot(kj, q_comp[c])
            if masked:
                sT = jnp.where(_causal(sT.shape), sT, MASK_VALUE)
            m_new = jnp.maximum(m, jnp.max(sT, axis=0, keepdims=True))
            alpha = jnp.exp2(m - m_new)
            p = jnp.exp2(sT - m_new)
            out.append((m_new, alpha * l + jnp.sum(p, axis=0, keepdims=True)))
            accs[c][...] = accs[c][...] * alpha + _dot(vj, p.astype(BF16))
        return tuple(out)

    init = ((jnp.full((1, blk), MASK_VALUE, F32), jnp.zeros((1, blk), F32)),) * 2
    carry = lax.fori_loop(0, qi, lambda j, c: block(j, c, False), init)
    (_, l0), (_, l1) = block(qi, carry, True)
    o_ref[0] = _attn_finish(lam_ref, gsub_ref, acc0_ref[...], acc1_ref[...], l0, l1)


def _attn_bounded_body(lam_ref, qT_ref, k_ref, vT_ref, gsub_ref, o_ref, acc_ref, sa_ref, sb_ref, *, blk):
    qi = pl.program_id(2)
    heads = tuple(slice(s * HEAD_COLS, (s + 1) * HEAD_COLS) for s in range(2))
    q_comp = [_split_components(qT_ref[0, 0, hs, :]) for hs in heads]
    acc_ref[...] = jnp.zeros_like(acc_ref)
    bufs = (sa_ref, sb_ref)

    half = blk // 2

    def scores(s, j, diagonal=False):
        kj = k_ref[0, pl.ds(pl.multiple_of(j * blk, blk), blk), heads[s]]
        for c in range(2):
            if diagonal:
                bufs[s][c, 0:half, :] = _dot(kj[0:half], q_comp[s][c])
                bufs[s][c, half:, half:] = _dot(kj[half:], q_comp[s][c][:, half:])
            else:
                bufs[s][c] = _dot(kj, q_comp[s][c])

    def consume(s, j, carry, masked):
        vj = vT_ref[0, j, heads[s], :]
        parts = list(carry)
        sum8 = lambda p: jnp.sum(p.reshape(p.shape[0] // SUBLANES, SUBLANES, p.shape[1]), axis=0)
        for c in range(2):
            if masked:
                p_lo = jnp.exp2(bufs[s][c, 0:half, :])
                p_lo = jnp.where(_causal(p_lo.shape), p_lo, 0.0)
                p_hi = jnp.exp2(bufs[s][c, half:, half:])
                p_hi = jnp.where(_causal(p_hi.shape), p_hi, 0.0)
                parts[2 * s + c] = (parts[2 * s + c] + sum8(p_lo)
                                    + jnp.concatenate([jnp.zeros((SUBLANES, half), F32), sum8(p_hi)], axis=1))
                acc_ref[s, c] += _dot(vj[:, 0:half], p_lo.astype(BF16))
                acc_ref[s, c, :, half:] += _dot(vj[:, half:], p_hi.astype(BF16))
            else:
                p = jnp.exp2(bufs[s][c])
                parts[2 * s + c] = parts[2 * s + c] + sum8(p)
                acc_ref[s, c] += _dot(vj, p.astype(BF16))
        return tuple(parts)

    def run(first, count, carry, diagonal_last):
        n_units = 2 * count
        for u in range(n_units):
            if u + 1 < n_units or not diagonal_last:
                scores((u + 1) % 2, first + (u + 1) // 2, diagonal_last and (u + 1) // 2 == count - 1)
            carry = consume(u % 2, first + u // 2, carry, diagonal_last and u // 2 == count - 1)
        return carry

    scores(0, 0)
    trips = qi // ATTN_UNROLL
    carry = lax.fori_loop(0, trips, lambda t, c: run(ATTN_UNROLL * t, ATTN_UNROLL, c, False),
                          (jnp.zeros((SUBLANES, blk), F32),) * 4)
    first = trips * ATTN_UNROLL
    tails = [functools.partial(run, first, rem + 1, diagonal_last=True) for rem in range(ATTN_UNROLL)]
    carry = lax.switch(qi - first, tails, carry)
    l = [jnp.sum(part, axis=0, keepdims=True) for part in carry]
    for s in range(2):
        o_ref[0, :, heads[s]] = _attn_finish(lam_ref, gsub_ref, acc_ref[s, 0], acc_ref[s, 1], l[2 * s], l[2 * s + 1])


def _attention(lam, qT, k, vT, gsub, *, bounded):
    bsz, nblk, d_qk, blk = qT.shape
    seq = k.shape[1]
    if bounded:
        group = 2
        body = functools.partial(_attn_bounded_body, blk=blk)
        scratch = [pltpu.VMEM((group, 2, V_DIM, blk), F32),
                   pltpu.VMEM((2, blk, blk), F32), pltpu.VMEM((2, blk, blk), F32)]
    else:
        group = 1
        body = functools.partial(_attn_online_body, blk=blk)
        scratch = [pltpu.VMEM((V_DIM, blk), F32), pltpu.VMEM((V_DIM, blk), F32)]
    return pl.pallas_call(
        body,
        name="attention_bounded" if bounded else "attention_online",
        grid=(bsz, N_HEADS // group, nblk),
        in_specs=[
            pl.BlockSpec(memory_space=pltpu.SMEM),
            pl.BlockSpec((1, 1, group * HEAD_COLS, blk), lambda b, h, i: (b, i, h, 0)),
            pl.BlockSpec((1, seq, group * HEAD_COLS), lambda b, h, i: (b, 0, h)),
            pl.BlockSpec((1, nblk, group * V_DIM, blk), lambda b, h, i: (b, 0, h, 0)),
            pl.BlockSpec((V_DIM, 1), lambda b, h, i: (0, 0)),
        ],
        out_specs=pl.BlockSpec((1, blk, group * V_DIM), lambda b, h, i: (b, i, h)),
        out_shape=jax.ShapeDtypeStruct((bsz, seq, N_HEADS * V_DIM), BF16),
        scratch_shapes=scratch,
        compiler_params=pltpu.CompilerParams(
            dimension_semantics=("arbitrary", "arbitrary", "arbitrary"), vmem_limit_bytes=VMEM_LIMIT),
    )(lam, qT, k, vT, gsub)


def _gelu_tanh(y):
    return 0.5 * y * (1.0 + jnp.tanh(math.sqrt(2.0 / math.pi) * (y + 0.044715 * (y * y * y))))


def _s5_expand_operators(kc_ref, wc_ref, oc_ref, rep_c_ref, rep_tc_ref, m_ref, win_ref, wout_ref):
    per = LANES // SSM_GROUP
    n = SSM_CHUNK * LANES
    row_group = (lax.broadcasted_iota(jnp.int32, (n, LANES), 0) // SSM_GROUP) % per
    lane_group = lax.broadcasted_iota(jnp.int32, (n, LANES), 1) // SSM_GROUP
    taps = jnp.where(row_group == lane_group, _dot(kc_ref[0], rep_c_ref[...]), 0.0).astype(BF16)
    zero = jnp.zeros((LANES, LANES), BF16)
    for s in range(SSM_CHUNK):
        for t in range(SSM_CHUNK):
            lag = t - s
            m_ref[s * LANES:(s + 1) * LANES, t * LANES:(t + 1) * LANES] = (
                taps[lag * LANES:(lag + 1) * LANES] if lag >= 0 else zero)
    wc = wc_ref[0].astype(F32)
    for j in range(per):
        win_ref[:, j * LANES:(j + 1) * LANES] = jnp.where(row_group == j, wc, 0.0).astype(BF16)
    oc = oc_ref[0]
    width = 2 * LANES
    state_group = lax.broadcasted_iota(jnp.int32, (oc.shape[0], width), 0) // (2 * SSM_STATE)
    out_group = (lax.broadcasted_iota(jnp.int32, (oc.shape[0], width), 1) // SSM_GROUP) % per
    for nb in range(n // width):
        cols = slice(nb * width, (nb + 1) * width)
        wout_ref[:, cols] = jnp.where(state_group == out_group, _dot(oc, rep_tc_ref[:, cols]), 0.0).astype(BF16)


def _s5_body(u_ref, kc_ref, wc_ref, oc_ref, rep_c_ref, rep_tc_ref, a1_ref, a2_ref, d_ref, y_ref,
             carry_ref, m_ref, win_ref, wout_ref, *, n_steps):
    @pl.when((pl.program_id(1) == 0) & (pl.program_id(2) == 0))
    def _():
        _s5_expand_operators(kc_ref, wc_ref, oc_ref, rep_c_ref, rep_tc_ref, m_ref, win_ref, wout_ref)

    @pl.when(pl.program_id(2) == 0)
    def _():
        carry_ref[...] = jnp.zeros_like(carry_ref)

    n_rows = u_ref.shape[1] // SSM_CHUNK
    slabs = [u_ref[0, pl.ds(s, n_rows, stride=SSM_CHUNK), :] for s in range(SSM_CHUNK)]
    ucat = jnp.concatenate([slab.astype(BF16) for slab in slabs], axis=1)
    x = _dot(ucat, win_ref[...])
    rows = lax.broadcasted_iota(jnp.int32, (n_rows, LANES), 0)
    x_prev = []
    for i in range(LANES // SSM_GROUP):
        lanes = slice(i * LANES, (i + 1) * LANES)
        a1 = a1_ref[0, :, lanes]
        a2 = a2_ref[0, :, lanes]
        x_in = carry_ref[:, lanes]

        def times_a(v, k):
            return a1[k:k + 1] * v + a2[k:k + 1] * pltpu.roll(v, SSM_STATE, axis=1)

        xi = x[:, lanes] + jnp.where(rows == 0, times_a(x_in, 0)[0:1], 0.0)
        for k in range(n_steps):
            xi = xi + times_a(jnp.where(rows >= (1 << k), pltpu.roll(xi, 1 << k, axis=0), 0.0), k)
        x_prev.append(jnp.where(rows == 0, x_in[0:1], pltpu.roll(xi, 1, axis=0)))
        carry_ref[:, lanes] = jnp.broadcast_to(xi[n_rows - 1:n_rows], (SUBLANES, LANES))
    x_prev = jnp.concatenate(x_prev, axis=1).astype(BF16)
    width = 2 * LANES
    for nb in range(SSM_CHUNK * LANES // width):
        cols = slice(nb * width, (nb + 1) * width)
        hi = (nb + 1) * width
        y = _dot(ucat[:, :hi], m_ref[:hi, cols]) + _dot(x_prev, wout_ref[:, cols])
        for t in range(2 * nb, 2 * nb + 2):
            yt = y[:, (t - 2 * nb) * LANES:(t - 2 * nb + 1) * LANES] + d_ref[0] * slabs[t]
            y_ref[0, pl.ds(t, n_rows, stride=SSM_CHUNK), :] = _gelu_tanh(yt)


def _s5(u, kc, wc, oc, a1, a2, d_t):
    bsz, seq, width = u.shape
    n_blocks = width // LANES
    tile = min(SSM_TILE, seq)
    n_steps = a1.shape[1]
    assert (1 << n_steps) == tile // SSM_CHUNK and seq % tile == 0
    n_in = SSM_CHUNK * LANES
    n_state = (LANES // SSM_GROUP) * 2 * SSM_STATE
    chan = np.arange(LANES) % SSM_GROUP
    rep_c = jnp.asarray(np.arange(SSM_GROUP)[:, None] == chan[None, :], BF16)
    tc_in = np.arange(SSM_CHUNK * SSM_GROUP)
    tc_out = (np.arange(n_in) // LANES) * SSM_GROUP + np.arange(n_in) % SSM_GROUP
    rep_tc = jnp.asarray(tc_in[:, None] == tc_out[None, :], BF16)
    body = functools.partial(_s5_body, n_steps=n_steps)
    per_block = lambda a: pl.BlockSpec((1,) + a.shape[1:], lambda v, b, i: (v, 0, 0))
    const = lambda a: pl.BlockSpec(a.shape, lambda v, b, i: (0, 0))
    tokens = pl.BlockSpec((1, tile, LANES), lambda v, b, i: (b, i, v))
    return pl.pallas_call(
        body,
        name="s5",
        grid=(n_blocks, bsz, seq // tile),
        in_specs=[tokens, per_block(kc), per_block(wc), per_block(oc), const(rep_c), const(rep_tc),
                  per_block(a1), per_block(a2), per_block(d_t)],
        out_specs=tokens,
        out_shape=jax.ShapeDtypeStruct(u.shape, F32),
        scratch_shapes=[pltpu.VMEM((SUBLANES, n_state), F32), pltpu.VMEM((n_in, n_in), BF16),
                        pltpu.VMEM((n_in, n_state), BF16), pltpu.VMEM((n_state, n_in), BF16)],
        compiler_params=pltpu.CompilerParams(
            dimension_semantics=("arbitrary", "arbitrary", "arbitrary"), vmem_limit_bytes=VMEM_LIMIT),
    )(u, kc, wc, oc, rep_c, rep_tc, a1, a2, d_t)


def _ssm_operators(a_re, a_im, log_dt, b_re, b_im, c_re, c_im, d_skip, n_steps):
    t_len = SSM_CHUNK
    groups, n_state = a_re.shape
    per = LANES // SSM_GROUP
    nv = groups // per
    hi = lax.Precision.HIGHEST
    dt = jnp.exp(log_dt)[:, None]
    mag = jnp.exp(a_re * dt)
    lb_re = mag * jnp.cos(a_im * dt)
    lb_im = mag * jnp.sin(a_im * dt)
    nr, ni = lb_re - 1.0, lb_im
    den = a_re * a_re + a_im * a_im
    cr = (nr * a_re + ni * a_im) / den
    ci = (ni * a_re - nr * a_im) / den
    bb_re = cr[..., None] * b_re - ci[..., None] * b_im
    bb_im = cr[..., None] * b_im + ci[..., None] * b_re

    def lam_pow(n):
        n = n.astype(F32)[:, None, None]
        m = jnp.exp(a_re * dt * n)
        return m * jnp.cos(a_im * dt * n), m * jnp.sin(a_im * dt * n)

    pw_re, pw_im = lam_pow(jnp.arange(t_len + 1))
    lbr = pw_re[..., None] * bb_re[None] - pw_im[..., None] * bb_im[None]
    lbi = pw_re[..., None] * bb_im[None] + pw_im[..., None] * bb_re[None]
    taps = (jnp.einsum('gcp,tgpd->tgcd', c_re, lbr[:t_len], precision=hi)
            - jnp.einsum('gcp,tgpd->tgcd', c_im, lbi[:t_len], precision=hi))
    kc = taps.reshape(t_len, nv, per, SSM_GROUP, SSM_GROUP).transpose(1, 0, 2, 4, 3)
    kc = kc.reshape(nv, t_len * LANES, SSM_GROUP).astype(BF16)
    lb = jnp.stack([lbr[:t_len][::-1], lbi[:t_len][::-1]], axis=0)
    lb = lb.reshape(2, t_len, nv, per, n_state, SSM_GROUP).transpose(2, 1, 3, 5, 0, 4)
    wc = lb.reshape(nv, t_len * LANES, 2 * n_state).astype(BF16)
    p_re = pw_re[1:, :, None, :]
    p_im = pw_im[1:, :, None, :]
    clr = c_re[None] * p_re - c_im[None] * p_im
    cli = c_re[None] * p_im + c_im[None] * p_re
    cl = jnp.stack([clr, -cli], axis=0)
    cl = cl.reshape(2, t_len, nv, per, SSM_GROUP, n_state).transpose(2, 3, 0, 5, 1, 4)
    oc = cl.reshape(nv, per * 2 * n_state, t_len * SSM_GROUP).astype(BF16)
    pa_re, pa_im = lam_pow(t_len * (2 ** jnp.arange(n_steps)))
    by_block = lambda a: a.reshape(n_steps, nv, per * 2 * n_state).transpose(1, 0, 2)
    a1 = by_block(jnp.concatenate([pa_re, pa_re], axis=-1))
    a2 = by_block(jnp.concatenate([-pa_im, pa_im], axis=-1))
    d_t = d_skip.reshape(nv, 1, LANES)
    return kc, wc, oc, a1, a2, d_t


def _combine_body(o_ref, gy_ref, sg_ref, x_ref, wap_ref, wglu_ref, bglu_ref, wsp_ref, wout_ref, g2_ref,
                  h_ref, hn_ref):
    attn = _dot(o_ref[0], wap_ref[...])
    g = gy_ref[0]
    s = g * jax.nn.sigmoid(_dot(g.astype(BF16), wglu_ref[...]) + bglu_ref[...])
    ssm = _dot(s.astype(BF16), wsp_ref[...])
    sg = sg_ref[0].astype(F32)
    d_model = attn.shape[1]
    mix = sg[:, :d_model] * attn + sg[:, d_model:] * ssm
    h = x_ref[0] + _dot(mix.astype(BF16), wout_ref[...])
    h_ref[0] = h
    hn_ref[0] = (h * _rms_scale(h) * g2_ref[...]).astype(BF16)


def _combine(o, gy, sg, x, wap, wglu, bglu, wsp, wout, g2):
    bsz, seq, d_model = x.shape
    tm = TOKEN_TILE
    d_u = gy.shape[-1]
    tile = lambda width: pl.BlockSpec((1, tm, width), lambda b, i: (b, i, 0))
    return pl.pallas_call(
        _combine_body,
        name="combine",
        grid=(bsz, seq // tm),
        in_specs=[tile(o.shape[-1]), tile(d_u), tile(2 * d_model), tile(d_model),
                  _resident(wap.shape), _resident(wglu.shape), _resident(bglu.shape),
                  _resident(wsp.shape), _resident(wout.shape), _resident(g2.shape)],
        out_specs=[tile(d_model), tile(d_model)],
        out_shape=[jax.ShapeDtypeStruct(x.shape, F32), jax.ShapeDtypeStruct(x.shape, BF16)],
        compiler_params=pltpu.CompilerParams(
            dimension_semantics=("arbitrary", "arbitrary"), vmem_limit_bytes=VMEM_LIMIT),
    )(o, gy, sg, x, wap, wglu, bglu, wsp, wout, g2)


def _ffn_body(hn_ref, h_ref, wup_ref, cw_ref, cb_ref, wdn_ref, y_ref, carry_ref, act_ref, upg_ref, upv_ref,
              *, d_ff):
    @pl.when(pl.program_id(1) == 0)
    def _():
        carry_ref[...] = jnp.zeros_like(carry_ref)

    hn = hn_ref[0]
    tm = hn.shape[0]

    def conv_cols(col0, up_ref):
        cols = pl.ds(col0, FFN_CHUNK)
        up = _dot(hn, wup_ref[:, cols])
        up_ref[0:SUBLANES, :] = carry_ref[:, cols]
        up_ref[SUBLANES:, :] = up
        carry_ref[:, cols] = up[tm - SUBLANES:, :]
        out = cb_ref[:, cols] + up * cw_ref[CONV_WIDTH - 1:CONV_WIDTH, cols]
        for tap in range(CONV_WIDTH - 1):
            lo = SUBLANES - (CONV_WIDTH - 1) + tap
            out = out + up_ref[lo:lo + tm, :] * cw_ref[tap:tap + 1, cols]
        return out

    for j in range(d_ff // FFN_CHUNK):
        gate = conv_cols(j * FFN_CHUNK, upg_ref)
        val = conv_cols(d_ff + j * FFN_CHUNK, upv_ref)
        act_ref[:, j * FFN_CHUNK:(j + 1) * FFN_CHUNK] = (gate * jax.nn.sigmoid(gate) * val).astype(BF16)
    y_ref[0] = h_ref[0] + _dot(act_ref[...], wdn_ref[...])


def _convffn(hn, h, wup, cw, cb, wdn):
    bsz, seq, d_model = h.shape
    tm = FFN_TILE
    d_ff = wdn.shape[0]
    assert d_ff % FFN_CHUNK == 0 and wup.shape[1] == 2 * d_ff
    tile = pl.BlockSpec((1, tm, d_model), lambda b, i: (b, i, 0))
    body = functools.partial(_ffn_body, d_ff=d_ff)
    return pl.pallas_call(
        body,
        name="convffn",
        grid=(bsz, seq // tm),
        in_specs=[tile, tile, _resident(wup.shape), _resident(cw.shape), _resident(cb.shape),
                  _resident(wdn.shape)],
        out_specs=tile,
        out_shape=jax.ShapeDtypeStruct(h.shape, F32),
        scratch_shapes=[pltpu.VMEM((SUBLANES, 2 * d_ff), F32), pltpu.VMEM((tm, d_ff), BF16),
                        pltpu.VMEM((SUBLANES + tm, FFN_CHUNK), F32), pltpu.VMEM((SUBLANES + tm, FFN_CHUNK), F32)],
        compiler_params=pltpu.CompilerParams(
            dimension_semantics=("arbitrary", "arbitrary"), vmem_limit_bytes=VMEM_LIMIT),
    )(hn, h, wup, cw, cb, wdn)


def kernel(x, norm1_gain, w_in, q_norm_gain, k_norm_gain, lambda_q1, lambda_k1, lambda_q2, lambda_k2,
           subln_gain, w_attn_proj, ssm_a_re, ssm_a_im, ssm_log_dt, ssm_b_re, ssm_b_im, ssm_c_re, ssm_c_im,
           ssm_d, w_glu, b_glu, w_ssm_proj, w_out, norm2_gain, w_up, conv_w, conv_b, w_down):
    bsz, seq, d_model = x.shape
    assert norm1_gain.shape[0] == 1, "single-layer trunk"
    assert seq % (TOKEN_TILE) == 0 and seq % SSM_CHUNK == 0
    d_qk = N_HEADS * HEAD_COLS

    w_in_b = _to_bf16(w_in[0], qk_cols=2 * d_qk)
    d_of_lane = _head_lane_dim()
    lane_gain = lambda g: jnp.tile(g[d_of_lane], N_HEADS)[None, :]
    gq = lane_gain(q_norm_gain[0].astype(F32)) * (LOG2E / math.sqrt(HEAD_DIM))
    gk = lane_gain(k_norm_gain[0].astype(F32))

    pos = np.arange(seq, dtype=np.float32)
    inv_freq = (1.0 / (np.float32(ROPE_THETA) ** (np.arange(0, HEAD_DIM, 2, dtype=np.float32) / np.float32(HEAD_DIM))))
    ang = pos[:, None] * inv_freq.astype(np.float32)[None, :]
    cos_t = jnp.asarray(np.tile(np.cos(ang), (1, HEAD_COLS // ROPE_HALF)), F32)
    sin_h = np.sin(ang)
    sin_t = jnp.asarray(np.concatenate([-sin_h, -sin_h, sin_h, sin_h], axis=1), F32)

    comp = np.tile(_head_lane_component(), N_HEADS)
    head = np.repeat(np.arange(N_HEADS), HEAD_COLS)
    e_np = np.zeros((d_qk, LANES), np.float32)
    e_np[np.arange(d_qk), head * 2 + comp] = 1.0
    e_mat = jnp.asarray(e_np, BF16)
    et_mat = jnp.asarray(np.concatenate([e_np.T, e_np.T], axis=0), BF16)

    qT, k, vT, u, sg = _inproj(x, norm1_gain.astype(F32), w_in_b, gq, gk, cos_t, sin_t, e_mat, et_mat)

    lam = (jnp.exp(jnp.sum(lambda_q1[0].astype(F32) * lambda_k1[0].astype(F32)))
           - jnp.exp(jnp.sum(lambda_q2[0].astype(F32) * lambda_k2[0].astype(F32))) + LAM_INIT)
    score_bound = HEAD_DIM * jnp.max(jnp.abs(gq)) * jnp.max(jnp.abs(gk))
    o = lax.cond(score_bound <= SCORE_LOG2_LIMIT,
                 functools.partial(_attention, bounded=True),
                 functools.partial(_attention, bounded=False),
                 lam.reshape(1, 1), qT, k, vT, subln_gain[0].astype(F32)[:, None])

    n_steps = int(math.log2(min(SSM_TILE, seq) // SSM_CHUNK))
    ops = _ssm_operators(ssm_a_re[0].astype(F32), ssm_a_im[0].astype(F32), ssm_log_dt[0].astype(F32),
                         ssm_b_re[0].astype(F32), ssm_b_im[0].astype(F32), ssm_c_re[0].astype(F32),
                         ssm_c_im[0].astype(F32), ssm_d[0].astype(F32), n_steps)
    gy = _s5(u, *ops)

    h, hn = _combine(o, gy, sg, x, _to_bf16(w_attn_proj[0]), _to_bf16(w_glu[0]),
                     b_glu.astype(F32), _to_bf16(w_ssm_proj[0]), _to_bf16(w_out[0]),
                     norm2_gain.astype(F32))
    return _convffn(hn, h, _to_bf16(w_up[0]), conv_w[0].astype(F32), conv_b.astype(F32),
                    _to_bf16(w_down[0]))
```

```python
import functools
import math

import numpy as np
import jax
import jax.numpy as jnp
from jax import lax
from jax.experimental import pallas as pl
from jax.experimental.pallas import tpu as pltpu

F32 = jnp.float32
BF16 = jnp.bfloat16

N_HEADS = 8
HEAD_DIM = 64
HEAD_COLS = 2 * HEAD_DIM
ROPE_HALF = HEAD_DIM // 2
V_DIM = 128
ROPE_THETA = 10000.0
SSM_GROUP = 16
SSM_GROUPS = 32
SSM_STATE = 64
SSM_CHUNK = 16
CONV_WIDTH = 3
EPS = 1e-6
LAM_INIT = 0.8 - 0.6 * math.exp(-0.3 * 0)

LANES = 128
SUBLANES = 8
TOKEN_TILE = 512
FFN_CHUNK = 256
FFN_TILE = 512
CAST_ROWS = 256
ATTN_UNROLL = 4
SSM_TILE = 4096
VMEM_LIMIT = 56 * 1024 * 1024
MASK_VALUE = -1e30
SCORE_LOG2_LIMIT = 40.0
LOG2E = math.log2(math.e)


def _dot(a, b):
    return jnp.dot(a, b, preferred_element_type=F32)


def _dot_rows(a, b):
    return lax.dot_general(a, b, (((0,), (0,)), ((), ())), preferred_element_type=F32)


def _rms_scale(x):
    return lax.rsqrt(jnp.mean(x * x, axis=-1, keepdims=True) + EPS)


def _resident(shape):
    return pl.BlockSpec(shape, lambda *_: (0,) * len(shape), pipeline_mode=pl.Buffered(1))


def _to_bf16_body(w_ref, o_ref, *, qk_cols):
    w = w_ref[...]
    if qk_cols:
        head = w[:, :qk_cols]
        quarter = (lax.broadcasted_iota(jnp.int32, head.shape, 1) % HEAD_COLS) // ROPE_HALF
        from_right = pltpu.roll(head, qk_cols - ROPE_HALF, axis=1)
        from_left = pltpu.roll(head, ROPE_HALF, axis=1)
        head = jnp.where(quarter == 1, from_right, jnp.where(quarter == 2, from_left, head))
        o_ref[:, :qk_cols] = head.astype(BF16)
        o_ref[:, qk_cols:] = w[:, qk_cols:].astype(BF16)
    else:
        o_ref[...] = w.astype(BF16)


def _to_bf16(w, qk_cols=0):
    rows, cols = w.shape
    blk = min(rows, CAST_ROWS)
    assert rows % blk == 0
    return pl.pallas_call(
        functools.partial(_to_bf16_body, qk_cols=qk_cols),
        name="to_bf16",
        grid=(rows // blk,),
        in_specs=[pl.BlockSpec((blk, cols), lambda i: (i, 0))],
        out_specs=pl.BlockSpec((blk, cols), lambda i: (i, 0)),
        out_shape=jax.ShapeDtypeStruct(w.shape, BF16),
        compiler_params=pltpu.CompilerParams(dimension_semantics=("arbitrary",), vmem_limit_bytes=VMEM_LIMIT),
    )(w)


def _head_lane_component():
    j = np.arange(HEAD_COLS)
    return (j // ROPE_HALF) % 2


def _head_lane_dim():
    j = np.arange(HEAD_COLS)
    return (j // HEAD_DIM) * ROPE_HALF + j % ROPE_HALF


def _inproj_body(x_ref, g1_ref, w_ref, gq_ref, gk_ref, cos_ref, sin_ref, e_ref, et_ref,
                 qT_ref, k_ref, vT_ref, u_ref, sg_ref, *, d_qk, d_v, d_u):
    x = x_ref[0]
    xn = (x * _rms_scale(x) * g1_ref[...]).astype(BF16)
    cos = cos_ref[...]
    sin = sin_ref[...]

    def norm_rope(col0, gain_ref):
        t = _dot(xn, w_ref[:, col0:col0 + d_qk])
        ss = _dot((t * t).astype(BF16), e_ref[...])
        r = lax.rsqrt(ss * (1.0 / HEAD_DIM) + EPS)
        r_hi = r.astype(BF16)
        r_lo = (r - r_hi.astype(F32)).astype(BF16)
        rf = _dot(jnp.concatenate([r_hi, r_lo], axis=1), et_ref[...])
        t = t * rf * gain_ref[...]
        heads = []
        for h in range(N_HEADS):
            th = t[:, h * HEAD_COLS:(h + 1) * HEAD_COLS]
            heads.append(th * cos + pltpu.roll(th, HEAD_DIM, axis=1) * sin)
        return jnp.concatenate(heads, axis=1)

    q = norm_rope(0, gq_ref)
    qT_ref[0, 0] = q.T.astype(BF16)
    k = norm_rope(d_qk, gk_ref)
    k_ref[0] = k.astype(BF16)
    c0 = 2 * d_qk
    v = _dot(xn, w_ref[:, c0:c0 + d_v])
    vT_ref[0, 0] = v.T.astype(BF16)
    c0 += d_v
    u_ref[0] = _dot(xn, w_ref[:, c0:c0 + d_u])
    c0 += d_u
    sg_ref[0] = jax.nn.sigmoid(_dot(xn, w_ref[:, c0:]))


def _inproj(x, g1, w_in_b, gq, gk, cos_t, sin_t, e_mat, et_mat):
    bsz, seq, d_model = x.shape
    tm = TOKEN_TILE
    nt = seq // tm
    d_qk = N_HEADS * HEAD_COLS
    d_v = N_HEADS * V_DIM
    d_u = SSM_GROUPS * SSM_GROUP
    d_g = 2 * d_model
    in_cols = w_in_b.shape[1]
    assert in_cols == 2 * d_qk + d_v + d_u + d_g and seq % tm == 0
    body = functools.partial(_inproj_body, d_qk=d_qk, d_v=d_v, d_u=d_u)
    return pl.pallas_call(
        body,
        name="inproj",
        grid=(bsz, nt),
        in_specs=[
            pl.BlockSpec((1, tm, d_model), lambda b, i: (b, i, 0)),
            _resident((1, d_model)),
            _resident((d_model, in_cols)),
            _resident((1, d_qk)),
            _resident((1, d_qk)),
            pl.BlockSpec((tm, LANES), lambda b, i: (i, 0)),
            pl.BlockSpec((tm, LANES), lambda b, i: (i, 0)),
            _resident((d_qk, LANES)),
            _resident((2 * LANES, d_qk)),
        ],
        out_specs=[
            pl.BlockSpec((1, 1, d_qk, tm), lambda b, i: (b, i, 0, 0)),
            pl.BlockSpec((1, tm, d_qk), lambda b, i: (b, i, 0)),
            pl.BlockSpec((1, 1, d_v, tm), lambda b, i: (b, i, 0, 0)),
            pl.BlockSpec((1, tm, d_u), lambda b, i: (b, i, 0)),
            pl.BlockSpec((1, tm, d_g), lambda b, i: (b, i, 0)),
        ],
        out_shape=[
            jax.ShapeDtypeStruct((bsz, nt, d_qk, tm), BF16),
            jax.ShapeDtypeStruct((bsz, seq, d_qk), BF16),
            jax.ShapeDtypeStruct((bsz, nt, d_v, tm), BF16),
            jax.ShapeDtypeStruct((bsz, seq, d_u), F32),
            jax.ShapeDtypeStruct((bsz, seq, d_g), F32),
        ],
        compiler_params=pltpu.CompilerParams(
            dimension_semantics=("arbitrary", "arbitrary"), vmem_limit_bytes=VMEM_LIMIT),
    )(x, g1, w_in_b, gq, gk, cos_t, sin_t, e_mat, et_mat)


def _split_components(qT):
    row = lax.broadcasted_iota(jnp.int32, qT.shape, 0)
    is_c0 = ((row // ROPE_HALF) % 2) == 0
    zero = jnp.zeros_like(qT)
    return jnp.where(is_c0, qT, zero), jnp.where(is_c0, zero, qT)


def _causal(shape):
    return lax.broadcasted_iota(jnp.int32, shape, 0) <= lax.broadcasted_iota(jnp.int32, shape, 1)


def _attn_finish(lam_ref, gsub_ref, acc0, acc1, l0, l1):
    lam = lam_ref[0, 0]
    a = acc0 * (1.0 / l0) - lam * (acc1 * (1.0 / l1))
    a = a * lax.rsqrt(jnp.mean(a * a, axis=0, keepdims=True) + EPS)
    a = a * gsub_ref[...] * (1.0 - LAM_INIT)
    return a.astype(BF16)


def _attn_online_body(lam_ref, qT_ref, k_ref, vT_ref, gsub_ref, o_ref, acc0_ref, acc1_ref, *, blk):
    qi = pl.program_id(2)
    q_comp = _split_components(qT_ref[0, 0])
    accs = (acc0_ref, acc1_ref)
    acc0_ref[...] = jnp.zeros_like(acc0_ref)
    acc1_ref[...] = jnp.zeros_like(acc1_ref)

    def block(j, carry, masked):
        kj = k_ref[0, pl.ds(pl.multiple_of(j * blk, blk), blk), :]
        vj = vT_ref[0, j]
        out = []
        for c in range(2):
            m, l = carry[c]
            sT = _dot(kj, q_comp[c])
            if masked:
                sT = jnp.where(_causal(sT.shape), sT, MASK_VALUE)
            m_new = jnp.maximum(m, jnp.max(sT, axis=0, keepdims=True))
            alpha = jnp.exp2(m - m_new)
            p = jnp.exp2(sT - m_new)
            out.append((m_new, alpha * l + jnp.sum(p, axis=0, keepdims=True)))
            accs[c][...] = accs[c][...] * alpha + _dot(vj, p.astype(BF16))
        return tuple(out)

    init = ((jnp.full((1, blk), MASK_VALUE, F32), jnp.zeros((1, blk), F32)),) * 2
    carry = lax.fori_loop(0, qi, lambda j, c: block(j, c, False), init)
    (_, l0), (_, l1) = block(qi, carry, True)
    o_ref[0, 0] = _attn_finish(lam_ref, gsub_ref, acc0_ref[...], acc1_ref[...], l0, l1)


def _attn_bounded_body(lam_ref, qT_ref, k_ref, vT_ref, gsub_ref, o_ref, acc_ref, sa_ref, sb_ref, *, blk):
    qi = pl.program_id(2)
    heads = tuple(slice(s * HEAD_COLS, (s + 1) * HEAD_COLS) for s in range(2))
    q_comp = [_split_components(qT_ref[0, 0, hs, :]) for hs in heads]
    acc_ref[...] = jnp.zeros_like(acc_ref)
    bufs = (sa_ref, sb_ref)

    half = blk // 2

    def scores(s, j, diagonal=False):
        kj = k_ref[0, pl.ds(pl.multiple_of(j * blk, blk), blk), heads[s]]
        for c in range(2):
            if diagonal:
                bufs[s][c, 0:half, :] = _dot(kj[0:half], q_comp[s][c])
                bufs[s][c, half:, half:] = _dot(kj[half:], q_comp[s][c][:, half:])
            else:
                bufs[s][c] = _dot(kj, q_comp[s][c])

    def consume(s, j, carry, masked):
        vj = vT_ref[0, j, heads[s], :]
        parts = list(carry)
        sum8 = lambda p: jnp.sum(p.reshape(p.shape[0] // SUBLANES, SUBLANES, p.shape[1]), axis=0)
        for c in range(2):
            if masked:
                p_lo = jnp.exp2(bufs[s][c, 0:half, :])
                p_lo = jnp.where(_causal(p_lo.shape), p_lo, 0.0)
                p_hi = jnp.exp2(bufs[s][c, half:, half:])
                p_hi = jnp.where(_causal(p_hi.shape), p_hi, 0.0)
                parts[2 * s + c] = (parts[2 * s + c] + sum8(p_lo)
                                    + jnp.concatenate([jnp.zeros((SUBLANES, half), F32), sum8(p_hi)], axis=1))
                acc_ref[s, c] += _dot(vj[:, 0:half], p_lo.astype(BF16))
                acc_ref[s, c, :, half:] += _dot(vj[:, half:], p_hi.astype(BF16))
            else:
                p = jnp.exp2(bufs[s][c])
                parts[2 * s + c] = parts[2 * s + c] + sum8(p)
                acc_ref[s, c] += _dot(vj, p.astype(BF16))
        return tuple(parts)

    def run(first, count, carry, diagonal_last):
        n_units = 2 * count
        for u in range(n_units):
            if u + 1 < n_units or not diagonal_last:
                scores((u + 1) % 2, first + (u + 1) // 2, diagonal_last and (u + 1) // 2 == count - 1)
            carry = consume(u % 2, first + u // 2, carry, diagonal_last and u // 2 == count - 1)
        return carry

    scores(0, 0)
    trips = qi // ATTN_UNROLL
    carry = lax.fori_loop(0, trips, lambda t, c: run(ATTN_UNROLL * t, ATTN_UNROLL, c, False),
                          (jnp.zeros((SUBLANES, blk), F32),) * 4)
    first = trips * ATTN_UNROLL
    tails = [functools.partial(run, first, rem + 1, diagonal_last=True) for rem in range(ATTN_UNROLL)]
    carry = lax.switch(qi - first, tails, carry)
    l = [jnp.sum(part, axis=0, keepdims=True) for part in carry]
    for s in range(2):
        o_ref[0, 0, heads[s], :] = _attn_finish(lam_ref, gsub_ref, acc_ref[s, 0], acc_ref[s, 1], l[2 * s], l[2 * s + 1])


def _attention(lam, qT, k, vT, gsub, *, bounded):
    bsz, nblk, d_qk, blk = qT.shape
    seq = k.shape[1]
    if bounded:
        group = 2
        body = functools.partial(_attn_bounded_body, blk=blk)
        scratch = [pltpu.VMEM((group, 2, V_DIM, blk), F32),
                   pltpu.VMEM((2, blk, blk), F32), pltpu.VMEM((2, blk, blk), F32)]
    else:
        group = 1
        body = functools.partial(_attn_online_body, blk=blk)
        scratch = [pltpu.VMEM((V_DIM, blk), F32), pltpu.VMEM((V_DIM, blk), F32)]
    return pl.pallas_call(
        body,
        name="attention_bounded" if bounded else "attention_online",
        grid=(bsz, N_HEADS // group, nblk),
        in_specs=[
            pl.BlockSpec(memory_space=pltpu.SMEM),
            pl.BlockSpec((1, 1, group * HEAD_COLS, blk), lambda b, h, i: (b, i, h, 0)),
            pl.BlockSpec((1, seq, group * HEAD_COLS), lambda b, h, i: (b, 0, h)),
            pl.BlockSpec((1, nblk, group * V_DIM, blk), lambda b, h, i: (b, 0, h, 0)),
            pl.BlockSpec((V_DIM, 1), lambda b, h, i: (0, 0)),
        ],
        out_specs=pl.BlockSpec((1, 1, group * V_DIM, blk), lambda b, h, i: (b, i, h, 0)),
        out_shape=jax.ShapeDtypeStruct((bsz, nblk, N_HEADS * V_DIM, blk), BF16),
        scratch_shapes=scratch,
        compiler_params=pltpu.CompilerParams(
            dimension_semantics=("arbitrary", "arbitrary", "arbitrary"), vmem_limit_bytes=VMEM_LIMIT),
    )(lam, qT, k, vT, gsub)


def _gelu_tanh(y):
    return 0.5 * y * (1.0 + jnp.tanh(math.sqrt(2.0 / math.pi) * (y + 0.044715 * (y * y * y))))


def _s5_expand_operators(kc_ref, wc_ref, oc_ref, rep_c_ref, rep_tc_ref, m_ref, win_ref, wout_ref):
    per = LANES // SSM_GROUP
    n = SSM_CHUNK * LANES
    row_group = (lax.broadcasted_iota(jnp.int32, (n, LANES), 0) // SSM_GROUP) % per
    lane_group = lax.broadcasted_iota(jnp.int32, (n, LANES), 1) // SSM_GROUP
    taps = jnp.where(row_group == lane_group, _dot(kc_ref[0], rep_c_ref[...]), 0.0).astype(BF16)
    zero = jnp.zeros((LANES, LANES), BF16)
    for s in range(SSM_CHUNK):
        for t in range(SSM_CHUNK):
            lag = t - s
            m_ref[s * LANES:(s + 1) * LANES, t * LANES:(t + 1) * LANES] = (
                taps[lag * LANES:(lag + 1) * LANES] if lag >= 0 else zero)
    wc = wc_ref[0].astype(F32)
    for j in range(per):
        win_ref[:, j * LANES:(j + 1) * LANES] = jnp.where(row_group == j, wc, 0.0).astype(BF16)
    oc = oc_ref[0]
    width = 2 * LANES
    state_group = lax.broadcasted_iota(jnp.int32, (oc.shape[0], width), 0) // (2 * SSM_STATE)
    out_group = (lax.broadcasted_iota(jnp.int32, (oc.shape[0], width), 1) // SSM_GROUP) % per
    for nb in range(n // width):
        cols = slice(nb * width, (nb + 1) * width)
        wout_ref[:, cols] = jnp.where(state_group == out_group, _dot(oc, rep_tc_ref[:, cols]), 0.0).astype(BF16)


def _s5_body(u_ref, kc_ref, wc_ref, oc_ref, rep_c_ref, rep_tc_ref, a1_ref, a2_ref, d_ref, y_ref,
             carry_ref, m_ref, win_ref, wout_ref, *, n_steps):
    @pl.when((pl.program_id(1) == 0) & (pl.program_id(2) == 0))
    def _():
        _s5_expand_operators(kc_ref, wc_ref, oc_ref, rep_c_ref, rep_tc_ref, m_ref, win_ref, wout_ref)

    @pl.when(pl.program_id(2) == 0)
    def _():
        carry_ref[...] = jnp.zeros_like(carry_ref)

    n_rows = u_ref.shape[1] // SSM_CHUNK
    slabs = [u_ref[0, pl.ds(s, n_rows, stride=SSM_CHUNK), :] for s in range(SSM_CHUNK)]
    ucat = jnp.concatenate([slab.astype(BF16) for slab in slabs], axis=1)
    x = _dot(ucat, win_ref[...])
    rows = lax.broadcasted_iota(jnp.int32, (n_rows, LANES), 0)
    x_prev = []
    for i in range(LANES // SSM_GROUP):
        lanes = slice(i * LANES, (i + 1) * LANES)
        a1 = a1_ref[0, :, lanes]
        a2 = a2_ref[0, :, lanes]
        x_in = carry_ref[:, lanes]

        def times_a(v, k):
            return a1[k:k + 1] * v + a2[k:k + 1] * pltpu.roll(v, SSM_STATE, axis=1)

        xi = x[:, lanes] + jnp.where(rows == 0, times_a(x_in, 0)[0:1], 0.0)
        for k in range(n_steps):
            xi = xi + times_a(jnp.where(rows >= (1 << k), pltpu.roll(xi, 1 << k, axis=0), 0.0), k)
        x_prev.append(jnp.where(rows == 0, x_in[0:1], pltpu.roll(xi, 1, axis=0)))
        carry_ref[:, lanes] = jnp.broadcast_to(xi[n_rows - 1:n_rows], (SUBLANES, LANES))
    x_prev = jnp.concatenate(x_prev, axis=1).astype(BF16)
    width = 2 * LANES
    for nb in range(SSM_CHUNK * LANES // width):
        cols = slice(nb * width, (nb + 1) * width)
        hi = (nb + 1) * width
        y = _dot(ucat[:, :hi], m_ref[:hi, cols]) + _dot(x_prev, wout_ref[:, cols])
        for t in range(2 * nb, 2 * nb + 2):
            yt = y[:, (t - 2 * nb) * LANES:(t - 2 * nb + 1) * LANES] + d_ref[0] * slabs[t]
            y_ref[0, pl.ds(t, n_rows, stride=SSM_CHUNK), :] = _gelu_tanh(yt)


def _s5(u, kc, wc, oc, a1, a2, d_t):
    bsz, seq, width = u.shape
    n_blocks = width // LANES
    tile = min(SSM_TILE, seq)
    n_steps = a1.shape[1]
    assert (1 << n_steps) == tile // SSM_CHUNK and seq % tile == 0
    n_in = SSM_CHUNK * LANES
    n_state = (LANES // SSM_GROUP) * 2 * SSM_STATE
    chan = np.arange(LANES) % SSM_GROUP
    rep_c = jnp.asarray(np.arange(SSM_GROUP)[:, None] == chan[None, :], BF16)
    tc_in = np.arange(SSM_CHUNK * SSM_GROUP)
    tc_out = (np.arange(n_in) // LANES) * SSM_GROUP + np.arange(n_in) % SSM_GROUP
    rep_tc = jnp.asarray(tc_in[:, None] == tc_out[None, :], BF16)
    body = functools.partial(_s5_body, n_steps=n_steps)
    per_block = lambda a: pl.BlockSpec((1,) + a.shape[1:], lambda v, b, i: (v, 0, 0))
    const = lambda a: pl.BlockSpec(a.shape, lambda v, b, i: (0, 0))
    tokens = pl.BlockSpec((1, tile, LANES), lambda v, b, i: (b, i, v))
    return pl.pallas_call(
        body,
        name="s5",
        grid=(n_blocks, bsz, seq // tile),
        in_specs=[tokens, per_block(kc), per_block(wc), per_block(oc), const(rep_c), const(rep_tc),
                  per_block(a1), per_block(a2), per_block(d_t)],
        out_specs=tokens,
        out_shape=jax.ShapeDtypeStruct(u.shape, F32),
        scratch_shapes=[pltpu.VMEM((SUBLANES, n_state), F32), pltpu.VMEM((n_in, n_in), BF16),
                        pltpu.VMEM((n_in, n_state), BF16), pltpu.VMEM((n_state, n_in), BF16)],
        compiler_params=pltpu.CompilerParams(
            dimension_semantics=("arbitrary", "arbitrary", "arbitrary"), vmem_limit_bytes=VMEM_LIMIT),
    )(u, kc, wc, oc, rep_c, rep_tc, a1, a2, d_t)


def _ssm_operators(a_re, a_im, log_dt, b_re, b_im, c_re, c_im, d_skip, n_steps):
    t_len = SSM_CHUNK
    groups, n_state = a_re.shape
    per = LANES // SSM_GROUP
    nv = groups // per
    hi = lax.Precision.HIGHEST
    dt = jnp.exp(log_dt)[:, None]
    mag = jnp.exp(a_re * dt)
    lb_re = mag * jnp.cos(a_im * dt)
    lb_im = mag * jnp.sin(a_im * dt)
    nr, ni = lb_re - 1.0, lb_im
    den = a_re * a_re + a_im * a_im
    cr = (nr * a_re + ni * a_im) / den
    ci = (ni * a_re - nr * a_im) / den
    bb_re = cr[..., None] * b_re - ci[..., None] * b_im
    bb_im = cr[..., None] * b_im + ci[..., None] * b_re

    def lam_pow(n):
        n = n.astype(F32)[:, None, None]
        m = jnp.exp(a_re * dt * n)
        return m * jnp.cos(a_im * dt * n), m * jnp.sin(a_im * dt * n)

    pw_re, pw_im = lam_pow(jnp.arange(t_len + 1))
    lbr = pw_re[..., None] * bb_re[None] - pw_im[..., None] * bb_im[None]
    lbi = pw_re[..., None] * bb_im[None] + pw_im[..., None] * bb_re[None]
    taps = (jnp.einsum('gcp,tgpd->tgcd', c_re, lbr[:t_len], precision=hi)
            - jnp.einsum('gcp,tgpd->tgcd', c_im, lbi[:t_len], precision=hi))
    kc = taps.reshape(t_len, nv, per, SSM_GROUP, SSM_GROUP).transpose(1, 0, 2, 4, 3)
    kc = kc.reshape(nv, t_len * LANES, SSM_GROUP).astype(BF16)
    lb = jnp.stack([lbr[:t_len][::-1], lbi[:t_len][::-1]], axis=0)
    lb = lb.reshape(2, t_len, nv, per, n_state, SSM_GROUP).transpose(2, 1, 3, 5, 0, 4)
    wc = lb.reshape(nv, t_len * LANES, 2 * n_state).astype(BF16)
    p_re = pw_re[1:, :, None, :]
    p_im = pw_im[1:, :, None, :]
    clr = c_re[None] * p_re - c_im[None] * p_im
    cli = c_re[None] * p_im + c_im[None] * p_re
    cl = jnp.stack([clr, -cli], axis=0)
    cl = cl.reshape(2, t_len, nv, per, SSM_GROUP, n_state).transpose(2, 3, 0, 5, 1, 4)
    oc = cl.reshape(nv, per * 2 * n_state, t_len * SSM_GROUP).astype(BF16)
    pa_re, pa_im = lam_pow(t_len * (2 ** jnp.arange(n_steps)))
    by_block = lambda a: a.reshape(n_steps, nv, per * 2 * n_state).transpose(1, 0, 2)
    a1 = by_block(jnp.concatenate([pa_re, pa_re], axis=-1))
    a2 = by_block(jnp.concatenate([-pa_im, pa_im], axis=-1))
    d_t = d_skip.reshape(nv, 1, LANES)
    return kc, wc, oc, a1, a2, d_t


def _combine_body(o_ref, gy_ref, sg_ref, x_ref, wap_ref, wglu_ref, bglu_ref, wsp_ref, wout_ref, g2_ref,
                  h_ref, hn_ref):
    attn = _dot_rows(o_ref[0, 0], wap_ref[...])
    g = gy_ref[0]
    s = g * jax.nn.sigmoid(_dot(g.astype(BF16), wglu_ref[...]) + bglu_ref[...])
    ssm = _dot(s.astype(BF16), wsp_ref[...])
    sg = sg_ref[0]
    d_model = attn.shape[1]
    mix = sg[:, :d_model] * attn + sg[:, d_model:] * ssm
    h = x_ref[0] + _dot(mix.astype(BF16), wout_ref[...])
    h_ref[0] = h
    hn_ref[0] = (h * _rms_scale(h) * g2_ref[...]).astype(BF16)


def _combine(o, gy, sg, x, wap, wglu, bglu, wsp, wout, g2):
    bsz, seq, d_model = x.shape
    tm = TOKEN_TILE
    d_u = gy.shape[-1]
    tile = lambda width: pl.BlockSpec((1, tm, width), lambda b, i: (b, i, 0))
    return pl.pallas_call(
        _combine_body,
        name="combine",
        grid=(bsz, seq // tm),
        in_specs=[pl.BlockSpec((1, 1, o.shape[2], tm), lambda b, i: (b, i, 0, 0)),
                  tile(d_u), tile(2 * d_model), tile(d_model),
                  _resident(wap.shape), _resident(wglu.shape), _resident(bglu.shape),
                  _resident(wsp.shape), _resident(wout.shape), _resident(g2.shape)],
        out_specs=[tile(d_model), tile(d_model)],
        out_shape=[jax.ShapeDtypeStruct(x.shape, F32), jax.ShapeDtypeStruct(x.shape, BF16)],
        compiler_params=pltpu.CompilerParams(
            dimension_semantics=("arbitrary", "arbitrary"), vmem_limit_bytes=VMEM_LIMIT),
    )(o, gy, sg, x, wap, wglu, bglu, wsp, wout, g2)


def _ffn_body(hn_ref, h_ref, wup_ref, cw_ref, cb_ref, wdn_ref, y_ref, carry_ref, act_ref, upg_ref, upv_ref,
              *, d_ff):
    @pl.when(pl.program_id(1) == 0)
    def _():
        carry_ref[...] = jnp.zeros_like(carry_ref)

    hn = hn_ref[0]
    tm = hn.shape[0]

    def conv_cols(col0, up_ref):
        cols = pl.ds(col0, FFN_CHUNK)
        up = _dot(hn, wup_ref[:, cols])
        up_ref[0:SUBLANES, :] = carry_ref[:, cols]
        up_ref[SUBLANES:, :] = up
        carry_ref[:, cols] = up[tm - SUBLANES:, :]
        out = cb_ref[:, cols] + up * cw_ref[CONV_WIDTH - 1:CONV_WIDTH, cols]
        for tap in range(CONV_WIDTH - 1):
            lo = SUBLANES - (CONV_WIDTH - 1) + tap
            out = out + up_ref[lo:lo + tm, :] * cw_ref[tap:tap + 1, cols]
        return out

    for j in range(d_ff // FFN_CHUNK):
        gate = conv_cols(j * FFN_CHUNK, upg_ref)
        val = conv_cols(d_ff + j * FFN_CHUNK, upv_ref)
        act_ref[:, j * FFN_CHUNK:(j + 1) * FFN_CHUNK] = (gate * jax.nn.sigmoid(gate) * val).astype(BF16)
    y_ref[0] = h_ref[0] + _dot(act_ref[...], wdn_ref[...])


def _convffn(hn, h, wup, cw, cb, wdn):
    bsz, seq, d_model = h.shape
    tm = FFN_TILE
    d_ff = wdn.shape[0]
    assert d_ff % FFN_CHUNK == 0 and wup.shape[1] == 2 * d_ff
    tile = pl.BlockSpec((1, tm, d_model), lambda b, i: (b, i, 0))
    body = functools.partial(_ffn_body, d_ff=d_ff)
    return pl.pallas_call(
        body,
        name="convffn",
        grid=(bsz, seq // tm),
        in_specs=[tile, tile, _resident(wup.shape), _resident(cw.shape), _resident(cb.shape),
                  _resident(wdn.shape)],
        out_specs=tile,
        out_shape=jax.ShapeDtypeStruct(h.shape, F32),
        scratch_shapes=[pltpu.VMEM((SUBLANES, 2 * d_ff), F32), pltpu.VMEM((tm, d_ff), BF16),
                        pltpu.VMEM((SUBLANES + tm, FFN_CHUNK), F32), pltpu.VMEM((SUBLANES + tm, FFN_CHUNK), F32)],
        compiler_params=pltpu.CompilerParams(
            dimension_semantics=("arbitrary", "arbitrary"), vmem_limit_bytes=VMEM_LIMIT),
    )(hn, h, wup, cw, cb, wdn)


def kernel(x, norm1_gain, w_in, q_norm_gain, k_norm_gain, lambda_q1, lambda_k1, lambda_q2, lambda_k2,
           subln_gain, w_attn_proj, ssm_a_re, ssm_a_im, ssm_log_dt, ssm_b_re, ssm_b_im, ssm_c_re, ssm_c_im,
           ssm_d, w_glu, b_glu, w_ssm_proj, w_out, norm2_gain, w_up, conv_w, conv_b, w_down):
    bsz, seq, d_model = x.shape
    assert norm1_gain.shape[0] == 1, "single-layer trunk"
    assert seq % (TOKEN_TILE) == 0 and seq % SSM_CHUNK == 0
    d_qk = N_HEADS * HEAD_COLS

    w_in_b = _to_bf16(w_in[0], qk_cols=2 * d_qk)
    d_of_lane = _head_lane_dim()
    lane_gain = lambda g: jnp.tile(g[d_of_lane], N_HEADS)[None, :]
    gq = lane_gain(q_norm_gain[0].astype(F32)) * (LOG2E / math.sqrt(HEAD_DIM))
    gk = lane_gain(k_norm_gain[0].astype(F32))

    pos = np.arange(seq, dtype=np.float32)
    inv_freq = (1.0 / (np.float32(ROPE_THETA) ** (np.arange(0, HEAD_DIM, 2, dtype=np.float32) / np.float32(HEAD_DIM))))
    ang = pos[:, None] * inv_freq.astype(np.float32)[None, :]
    cos_t = jnp.asarray(np.tile(np.cos(ang), (1, HEAD_COLS // ROPE_HALF)), F32)
    sin_h = np.sin(ang)
    sin_t = jnp.asarray(np.concatenate([-sin_h, -sin_h, sin_h, sin_h], axis=1), F32)

    comp = np.tile(_head_lane_component(), N_HEADS)
    head = np.repeat(np.arange(N_HEADS), HEAD_COLS)
    e_np = np.zeros((d_qk, LANES), np.float32)
    e_np[np.arange(d_qk), head * 2 + comp] = 1.0
    e_mat = jnp.asarray(e_np, BF16)
    et_mat = jnp.asarray(np.concatenate([e_np.T, e_np.T], axis=0), BF16)

    qT, k, vT, u, sg = _inproj(x, norm1_gain.astype(F32), w_in_b, gq, gk, cos_t, sin_t, e_mat, et_mat)

    lam = (jnp.exp(jnp.sum(lambda_q1[0].astype(F32) * lambda_k1[0].astype(F32)))
           - jnp.exp(jnp.sum(lambda_q2[0].astype(F32) * lambda_k2[0].astype(F32))) + LAM_INIT)
    score_bound = HEAD_DIM * jnp.max(jnp.abs(gq)) * jnp.max(jnp.abs(gk))
    o = lax.cond(score_bound <= SCORE_LOG2_LIMIT,
                 functools.partial(_attention, bounded=True),
                 functools.partial(_attention, bounded=False),
                 lam.reshape(1, 1), qT, k, vT, subln_gain[0].astype(F32)[:, None])

    n_steps = int(math.log2(min(SSM_TILE, seq) // SSM_CHUNK))
    ops = _ssm_operators(ssm_a_re[0].astype(F32), ssm_a_im[0].astype(F32), ssm_log_dt[0].astype(F32),
                         ssm_b_re[0].astype(F32), ssm_b_im[0].astype(F32), ssm_c_re[0].astype(F32),
                         ssm_c_im[0].astype(F32), ssm_d[0].astype(F32), n_steps)
    gy = _s5(u, *ops)

    h, hn = _combine(o, gy, sg, x, _to_bf16(w_attn_proj[0]), _to_bf16(w_glu[0]),
                     b_glu.astype(F32), _to_bf16(w_ssm_proj[0]), _to_bf16(w_out[0]),
                     norm2_gain.astype(F32))
    return _convffn(hn, h, _to_bf16(w_up[0]), conv_w[0].astype(F32), conv_b.astype(F32),
                    _to_bf16(w_down[0]))
```
